```python
import jax, jax.numpy as jnp
from jax import lax
import numpy as np

D_MODEL = 1024
BATCH = 8
SEQ = 2048
DEPTH = 1

CHUNK = 64
RET_HEADS = 8
RET_DQK = 64
RET_DV = 128
SB_HEADS = 8
SB_DH = 64
SB_BLOCK = 128
D_FF = 4 * D_MODEL
ROPE_BASE = 10000.0
EPS = 1e-6

RET_QK = RET_HEADS * RET_DQK
RET_V = RET_HEADS * RET_DV
SB_W = SB_HEADS * SB_DH
IN_SPLITS = (RET_QK, RET_QK, RET_V, RET_V, SB_W, SB_W, SB_W, D_MODEL, D_MODEL)
D_IN = 2 * RET_QK + 2 * RET_V + 3 * SB_W + 2 * D_MODEL

kernel_name = "hybrid_retention_stickbreaking_block"


def rmsnorm(x, g):
    xf = x.astype(jnp.float32)
    y = xf * lax.rsqrt(jnp.mean(xf * xf, axis=-1, keepdims=True) + EPS)
    return y * g


def modulate(h, shift, scale):
    return h * (1.0 + scale[:, None, :]) + shift[:, None, :]


def rotary(x, pos):
    d = x.shape[-1]
    inv_freq = ROPE_BASE ** (-jnp.arange(0, d, 2, dtype=jnp.float32) / d)
    ang = pos.astype(jnp.float32)[..., None] * inv_freq
    cos = jnp.cos(ang)[:, :, None, :]
    sin = jnp.sin(ang)[:, :, None, :]
    x1, x2 = x[..., : d // 2], x[..., d // 2:]
    return jnp.concatenate([x1 * cos - x2 * sin, x1 * sin + x2 * cos], axis=-1)


def retention(q, k, v, pos):
    B, S = q.shape[0], q.shape[1]
    nc = S // CHUNK
    q = rotary(q.astype(jnp.float32), pos)
    k = rotary(k.astype(jnp.float32), pos) * (RET_DQK ** -0.5)
    v = v.astype(jnp.float32)
    log_gamma = jnp.log1p(-(2.0 ** (-5.0 - jnp.arange(RET_HEADS, dtype=jnp.float32))))
    qc = q.reshape(B, nc, CHUNK, RET_HEADS, RET_DQK)
    kc = k.reshape(B, nc, CHUNK, RET_HEADS, RET_DQK)
    vc = v.reshape(B, nc, CHUNK, RET_HEADS, RET_DV)
    idx = jnp.arange(CHUNK, dtype=jnp.float32)
    intra_decay = jnp.exp(jnp.abs(idx[:, None] - idx[None, :])[None] * log_gamma[:, None, None])
    scores = jnp.einsum('bnihd,bnjhd->bnhij', qc, kc) * intra_decay
    intra = jnp.einsum('bnhij,bnjhe->bnihe', scores, vc)
    k_to_end = jnp.exp((CHUNK - 1.0 - idx)[:, None] * log_gamma[None, :])
    kv = jnp.einsum('bnjhd,jh,bnjhe->nbhde', kc, k_to_end, vc)
    chunk_decay = jnp.exp(CHUNK * log_gamma)[None, :, None, None]

    def step(state, kv_n):
        return state * chunk_decay + kv_n, state

    _, s_prev = lax.scan(step, jnp.zeros(kv.shape[1:], jnp.float32), kv)
    q_from_start = jnp.exp((idx + 1.0)[:, None] * log_gamma[None, :])
    cross = jnp.einsum('bnihd,ih,nbhde->bnihe', qc, q_from_start, s_prev)
    return (intra + cross).reshape(B, S, RET_HEADS, RET_DV)


def stick_breaking(q, k, v):
    S = q.shape[1]
    scale = SB_DH ** -0.5
    q = q.astype(jnp.float32)
    k = k.astype(jnp.float32)
    v = v.astype(jnp.float32)
    outs = []
    for blk in range(S // SB_BLOCK):
        t0 = blk * SB_BLOCK
        t1 = t0 + SB_BLOCK
        z = jnp.einsum('bthd,bshd->bhts', q[:, t0:t1], k[:, :t1]) * scale
        t_idx = t0 + jnp.arange(SB_BLOCK)
        s_idx = jnp.arange(t1)
        valid = s_idx[None, :] < t_idx[:, None]
        log_1m = jnp.where(valid, -jax.nn.softplus(z), 0.0)
        log_stick = lax.cumsum(log_1m, axis=3, reverse=True) - log_1m
        a = jnp.where(valid, jnp.exp(jax.nn.log_sigmoid(z) + log_stick), 0.0)
        outs.append(jnp.einsum('bhts,bshd->bthd', a, v[:, :t1]))
    return jnp.concatenate(outs, axis=1)


def setup_inputs(seed: int = 0) -> dict:
    key = jax.random.key(seed)
    ks = jax.random.split(key, 18)
    nrm = jax.random.normal
    f32 = jnp.float32
    x = nrm(ks[0], (BATCH, SEQ, D_MODEL), f32)
    c = nrm(ks[1], (BATCH, D_MODEL), f32)
    offset = jax.random.randint(ks[2], (BATCH, 1), 0, 256, dtype=jnp.int32) * CHUNK
    positions = (offset + jnp.arange(SEQ, dtype=jnp.int32)[None, :]).astype(jnp.int32)
    ada_w = nrm(ks[3], (DEPTH, D_MODEL, 6 * D_MODEL), f32) * D_MODEL ** -0.5
    ada_b = 0.01 * nrm(ks[4], (DEPTH, 6 * D_MODEL), f32)
    pre_mix_g = 1.0 + 0.02 * nrm(ks[5], (DEPTH, D_MODEL), f32)
    post_mix_g = 1.0 + 0.02 * nrm(ks[6], (DEPTH, D_MODEL), f32)
    pre_ffn_g = 1.0 + 0.02 * nrm(ks[7], (DEPTH, D_MODEL), f32)
    post_ffn_g = 1.0 + 0.02 * nrm(ks[8], (DEPTH, D_MODEL), f32)
    w_in = nrm(ks[9], (DEPTH, D_MODEL, D_IN), f32) * D_MODEL ** -0.5
    ret_gn_g = 1.0 + 0.02 * nrm(ks[10], (DEPTH, RET_V), f32)
    w_ret_branch = nrm(ks[11], (DEPTH, RET_V, D_MODEL), f32) * RET_V ** -0.5
    w_sb_branch = nrm(ks[12], (DEPTH, SB_W, D_MODEL), f32) * SB_W ** -0.5
    w_out = nrm(ks[13], (DEPTH, D_MODEL, D_MODEL), f32) * D_MODEL ** -0.5
    w_ff1 = nrm(ks[14], (DEPTH, D_MODEL, D_FF), f32) * D_MODEL ** -0.5
    w_ff2 = nrm(ks[15], (DEPTH, D_FF, D_MODEL), f32) * D_FF ** -0.5
    return {"x": x, "c": c, "positions": positions, "ada_w": ada_w, "ada_b": ada_b,
            "pre_mix_g": pre_mix_g, "post_mix_g": post_mix_g, "pre_ffn_g": pre_ffn_g,
            "post_ffn_g": post_ffn_g, "w_in": w_in, "ret_gn_g": ret_gn_g,
            "w_ret_branch": w_ret_branch, "w_sb_branch": w_sb_branch, "w_out": w_out,
            "w_ff1": w_ff1, "w_ff2": w_ff2}


def reference(x, c, positions, ada_w, ada_b, pre_mix_g, post_mix_g, pre_ffn_g, post_ffn_g,
              w_in, ret_gn_g, w_ret_branch, w_sb_branch, w_out, w_ff1, w_ff2):
    out_dtype = x.dtype
    B, S, _ = x.shape
    split_pts = [int(p) for p in np.cumsum(IN_SPLITS)[:-1]]
    h_res = x.astype(jnp.float32)
    for l in range(DEPTH):
        mod = jax.nn.silu(c.astype(jnp.float32)) @ ada_w[l] + ada_b[l]
        sh1, sc1, gt1, sh2, sc2, gt2 = jnp.split(mod, 6, axis=-1)

        h = modulate(rmsnorm(h_res, pre_mix_g[l]), sh1, sc1)
        proj = h @ w_in[l]
        q_r, k_r, v_r, g_r, q_s, k_s, v_s, a_r, a_s = jnp.split(proj, split_pts, axis=-1)

        ret = retention(q_r.reshape(B, S, RET_HEADS, RET_DQK),
                        k_r.reshape(B, S, RET_HEADS, RET_DQK),
                        v_r.reshape(B, S, RET_HEADS, RET_DV), positions)
        mu = jnp.mean(ret, axis=-1, keepdims=True)
        var = jnp.mean(jnp.square(ret - mu), axis=-1, keepdims=True)
        ret = (ret - mu) * lax.rsqrt(var + EPS) * ret_gn_g[l].reshape(RET_HEADS, RET_DV)
        ret = jax.nn.silu(g_r) * ret.reshape(B, S, RET_V)

        sb = stick_breaking(q_s.reshape(B, S, SB_HEADS, SB_DH),
                            k_s.reshape(B, S, SB_HEADS, SB_DH),
                            v_s.reshape(B, S, SB_HEADS, SB_DH)).reshape(B, S, SB_W)

        mixed = (jax.nn.sigmoid(a_r) * (ret @ w_ret_branch[l])
                 + jax.nn.sigmoid(a_s) * (sb @ w_sb_branch[l]))
        y = mixed @ w_out[l]
        h_res = h_res + gt1[:, None, :] * rmsnorm(y, post_mix_g[l])

        h2 = modulate(rmsnorm(h_res, pre_ffn_g[l]), sh2, sc2)
        f = jnp.square(jax.nn.relu(h2 @ w_ff1[l])) @ w_ff2[l]
        h_res = h_res + gt2[:, None, :] * rmsnorm(f, post_ffn_g[l])
    return h_res.astype(out_dtype)
```

```python
import functools

import jax
import jax.numpy as jnp
import numpy as np
from jax import lax
from jax.experimental import pallas as pl
from jax.experimental.pallas import tpu as pltpu

D_MODEL = 1024
CHUNK = 64
RET_HEADS = 8
RET_DQK = 64
RET_DV = 128
SB_HEADS = 8
SB_DH = 64
D_FF = 4 * D_MODEL
ROPE_BASE = 10000.0
EPS = 1e-6

RET_QK = RET_HEADS * RET_DQK
RET_V = RET_HEADS * RET_DV
SB_W = SB_HEADS * SB_DH
D_IN = 2 * RET_QK + 2 * RET_V + 3 * SB_W + 2 * D_MODEL

LANES = 128
VMEM_LIMIT = 56 * 1024 * 1024

TM_IN = 512
TM_POST = 512
RET_L = 256
SB_T = 256

F32 = jnp.float32
BF16 = jnp.bfloat16


def _dot(a, b):
    return jnp.dot(a, b, preferred_element_type=F32)


def _dot_nt(a, b):
    return lax.dot_general(a, b, (((1,), (1,)), ((), ())), preferred_element_type=F32)


def _dot_tn(a, b):
    return lax.dot_general(a, b, (((0,), (0,)), ((), ())), preferred_element_type=F32)


def _rms(x, g):
    return x * lax.rsqrt(jnp.mean(x * x, axis=-1, keepdims=True) + EPS) * g


def _resident(shape):
    nd = len(shape)
    return pl.BlockSpec(shape, lambda *_: (0,) * nd, pipeline_mode=pl.Buffered(1))


def _ada_kernel(c_ref, w_ref, b_ref, o_ref):
    c = c_ref[...]
    s = (c * jax.nn.sigmoid(c)).astype(BF16)
    o_ref[...] = _dot(s, w_ref[...].astype(BF16)) + b_ref[...]


def _ada(c, w, b):
    bsz, d = c.shape
    n = w.shape[1]
    tn = 1024
    return pl.pallas_call(
        _ada_kernel,
        out_shape=jax.ShapeDtypeStruct((bsz, n), F32),
        grid=(n // tn,),
        in_specs=[pl.BlockSpec((bsz, d), lambda j: (0, 0)),
                  pl.BlockSpec((d, tn), lambda j: (0, j)),
                  pl.BlockSpec((1, tn), lambda j: (0, j))],
        out_specs=pl.BlockSpec((bsz, tn), lambda j: (0, j)),
        compiler_params=pltpu.CompilerParams(dimension_semantics=("arbitrary",)),
        name="ada",
    )(c, w, b.reshape(1, n))


def _inproj_kernel(x_ref, mod_ref, pos_ref, g_ref, freq_ref, sgn_ref, w_ref,
                   qr_ref, kr_ref, vr_ref, gr_ref, qs_ref, ks_ref, vs_ref, ar_ref, as_ref):
    h = _rms(x_ref[...], g_ref[...])
    h = h * (1.0 + mod_ref[0, 1:2, :]) + mod_ref[0, 0:1, :]
    hb = h.astype(BF16)

    ang = pos_ref[...].astype(F32) * freq_ref[...]
    cos = jnp.cos(ang)
    sin = jnp.sin(ang) * sgn_ref[...]
    lane = lax.broadcasted_iota(jnp.int32, (1, LANES), 1)
    first_half = (lane % RET_DQK) < (RET_DQK // 2)

    def rope(p):
        rot = jnp.where(first_half, pltpu.roll(p, LANES - RET_DQK // 2, 1),
                        pltpu.roll(p, RET_DQK // 2, 1))
        return p * cos + rot * sin

    def proj(c0, width):
        return _dot(hb, w_ref[:, c0:c0 + width])

    c0 = 0
    acc = proj(c0, RET_QK)
    for j in range(RET_QK // LANES):
        qr_ref[:, j * LANES:(j + 1) * LANES] = rope(acc[:, j * LANES:(j + 1) * LANES]).astype(BF16)
    c0 += RET_QK
    acc = proj(c0, RET_QK)
    kscale = RET_DQK ** -0.5
    for j in range(RET_QK // LANES):
        kr_ref[:, j * LANES:(j + 1) * LANES] = (
            rope(acc[:, j * LANES:(j + 1) * LANES]) * kscale).astype(BF16)
    c0 += RET_QK
    for ref, width in ((vr_ref, RET_V), (gr_ref, RET_V), (qs_ref, SB_W), (ks_ref, SB_W),
                       (vs_ref, SB_W), (ar_ref, D_MODEL), (as_ref, D_MODEL)):
        ref[...] = proj(c0, width).astype(BF16)
        c0 += width


def _inproj(x2, mod3, pos2, g, freq, sgn, w_bf, seq):
    t, d = x2.shape
    tm = TM_IN
    tiles_per_seq = seq // tm
    widths = (RET_QK, RET_QK, RET_V, RET_V, SB_W, SB_W, SB_W, D_MODEL, D_MODEL)
    row = lambda i: (i, 0)
    return pl.pallas_call(
        _inproj_kernel,
        out_shape=[jax.ShapeDtypeStruct((t, w), BF16) for w in widths],
        grid=(t // tm,),
        in_specs=[pl.BlockSpec((tm, d), row),
                  pl.BlockSpec((1, 6, d), lambda i: (i // tiles_per_seq, 0, 0)),
                  pl.BlockSpec((tm, 1), row),
                  _resident((1, d)),
                  _resident((1, LANES)),
                  _resident((1, LANES)),
                  _resident((d, D_IN))],
        out_specs=[pl.BlockSpec((tm, w), row) for w in widths],
        compiler_params=pltpu.CompilerParams(dimension_semantics=("arbitrary",),
                                             vmem_limit_bytes=VMEM_LIMIT),
        name="inproj",
    )(x2, mod3, pos2, g, freq, sgn, w_bf)


def _ret_kernel(lg_ref, q_ref, k_ref, v_ref, g_ref, gn_ref, o_ref, state_ref, dtab_ref):
    b = pl.program_id(0)
    j = pl.program_id(1)
    blk = RET_L

    @pl.when(jnp.logical_and(b == 0, j == 0))
    def _():
        ri = lax.broadcasted_iota(jnp.int32, (blk, blk), 0)
        ci = lax.broadcasted_iota(jnp.int32, (blk, blk), 1)
        dist = jnp.abs(ri - ci).astype(F32)
        keep = (ci // CHUNK) <= (ri // CHUNK)
        for h in range(RET_HEADS):
            dtab_ref[h] = jnp.where(keep, jnp.exp(dist * lg_ref[h]), 0.0)

    @pl.when(j == 0)
    def _():
        state_ref[...] = jnp.zeros_like(state_ref)

    lane = lax.broadcasted_iota(jnp.int32, (1, LANES), 1)
    idx = lax.broadcasted_iota(jnp.int32, (blk, LANES), 0).astype(F32)
    for h in range(RET_HEADS):
        p = h // 2
        lg = lg_ref[h]
        in_head = (lane // RET_DQK) == (h % 2)
        qp = q_ref[:, p * LANES:(p + 1) * LANES]
        kp = k_ref[:, p * LANES:(p + 1) * LANES]
        vh = v_ref[:, h * RET_DV:(h + 1) * RET_DV]
        qh = jnp.where(in_head, qp, jnp.zeros_like(qp))
        st = state_ref[h]

        scores = _dot_nt(qh, kp) * dtab_ref[h]
        intra = _dot(scores.astype(BF16), vh)
        cross = _dot(qh, st.astype(BF16)) * jnp.exp((idx + 1.0) * lg)
        kd = (kp.astype(F32) * jnp.exp((blk - 1.0 - idx) * lg)).astype(BF16)
        state_ref[h] = st * jnp.exp(blk * lg) + _dot_tn(kd, vh)

        o = intra + cross
        mu = jnp.mean(o, axis=-1, keepdims=True)
        dlt = o - mu
        var = jnp.mean(dlt * dlt, axis=-1, keepdims=True)
        y = dlt * lax.rsqrt(var + EPS) * gn_ref[:, h * RET_DV:(h + 1) * RET_DV]
        g = g_ref[:, h * RET_DV:(h + 1) * RET_DV].astype(F32)
        o_ref[:, h * RET_DV:(h + 1) * RET_DV] = (g * jax.nn.sigmoid(g) * y).astype(BF16)


def _retention(log_gamma, q, k, v, g, gn, bsz, seq):
    t = q.shape[0]
    nb = seq // RET_L
    row = lambda b, j: (b * nb + j, 0)
    return pl.pallas_call(
        _ret_kernel,
        out_shape=jax.ShapeDtypeStruct((t, RET_V), BF16),
        grid=(bsz, nb),
        in_specs=[pl.BlockSpec(memory_space=pltpu.SMEM),
                  pl.BlockSpec((RET_L, RET_QK), row),
                  pl.BlockSpec((RET_L, RET_QK), row),
                  pl.BlockSpec((RET_L, RET_V), row),
                  pl.BlockSpec((RET_L, RET_V), row),
                  pl.BlockSpec((1, RET_V), lambda b, j: (0, 0))],
        out_specs=pl.BlockSpec((RET_L, RET_V), row),
        scratch_shapes=[pltpu.VMEM((RET_HEADS, LANES, RET_DV), F32),
                        pltpu.VMEM((RET_HEADS, RET_L, RET_L), F32)],
        compiler_params=pltpu.CompilerParams(dimension_semantics=("arbitrary", "arbitrary"),
                                             vmem_limit_bytes=VMEM_LIMIT),
        name="retention",
    )(log_gamma, q, k, v, g, gn)


def _sb_kernel(q_ref, k_ref, v_ref, o_ref):
    qi = pl.program_id(1)
    t = SB_T
    lane = lax.broadcasted_iota(jnp.int32, (1, LANES), 1)
    ri = lax.broadcasted_iota(jnp.int32, (t, t), 0)
    ci = lax.broadcasted_iota(jnp.int32, (t, t), 1)
    valid = ci < ri
    tri = jnp.where(ri > ci, 1.0, 0.0).astype(BF16)
    scale = SB_DH ** -0.5

    def rev_cumsum_excl(lp):
        hi = lp.astype(BF16)
        lo = (lp - hi.astype(F32)).astype(BF16)
        return _dot(hi, tri) + _dot(lo, tri)

    def softplus(z):
        return jnp.maximum(z, 0.0) + jnp.log1p(jnp.exp(-jnp.abs(z)))

    for p in range(SB_HEADS // 2):
        cols = slice(p * LANES, (p + 1) * LANES)
        qp = q_ref[:, cols]
        accs = []
        for hh in range(2):
            in_head = (lane // SB_DH) == hh
            qh = jnp.where(in_head, qp, jnp.zeros_like(qp)) * scale

            start = pl.multiple_of(qi * t, t)
            z = _dot_nt(qh, k_ref[pl.ds(start, t), cols])
            sp = softplus(z)
            lp = jnp.where(valid, -sp, 0.0)
            a = jnp.where(valid, jnp.exp(z - sp + rev_cumsum_excl(lp)), 0.0)
            acc = _dot(a.astype(BF16), v_ref[pl.ds(start, t), cols])
            r = jnp.sum(lp, axis=-1, keepdims=True)

            def body(it, carry):
                acc, r = carry
                start = pl.multiple_of((qi - 1 - it) * t, t)
                z = _dot_nt(qh, k_ref[pl.ds(start, t), cols])
                sp = softplus(z)
                lp = -sp
                a = jnp.exp(z - sp + rev_cumsum_excl(lp) + r)
                acc = acc + _dot(a.astype(BF16), v_ref[pl.ds(start, t), cols])
                return acc, r + jnp.sum(lp, axis=-1, keepdims=True)

            acc, _ = lax.fori_loop(0, qi, body, (acc, r))
            accs.append(acc)
        o_ref[:, cols] = jnp.where(lane < SB_DH, accs[0], accs[1]).astype(BF16)


def _stick_breaking(q, k, v, bsz, seq):
    t = q.shape[0]
    nq = seq // SB_T
    return pl.pallas_call(
        _sb_kernel,
        out_shape=jax.ShapeDtypeStruct((t, SB_W), BF16),
        grid=(bsz, nq),
        in_specs=[pl.BlockSpec((SB_T, SB_W), lambda b, i: (b * nq + i, 0)),
                  pl.BlockSpec((seq, SB_W), lambda b, i: (b, 0)),
                  pl.BlockSpec((seq, SB_W), lambda b, i: (b, 0))],
        out_specs=pl.BlockSpec((SB_T, SB_W), lambda b, i: (b * nq + i, 0)),
        compiler_params=pltpu.CompilerParams(dimension_semantics=("arbitrary", "arbitrary"),
                                             vmem_limit_bytes=VMEM_LIMIT),
        name="stickbreak",
    )(q, k, v)


def _post_kernel(x_ref, mod_ref, ret_ref, sb_ref, ar_ref, as_ref,
                 g_mix_ref, g_pre_ref, g_post_ref,
                 w_ret_ref, w_sb_ref, w_out_ref, w_ff1_ref, w_ff2_ref, o_ref):
    gt1 = mod_ref[0, 2:3, :]
    sh2 = mod_ref[0, 3:4, :]
    sc2 = mod_ref[0, 4:5, :]
    gt2 = mod_ref[0, 5:6, :]

    mixed = (jax.nn.sigmoid(ar_ref[...].astype(F32)) * _dot(ret_ref[...], w_ret_ref[...])
             + jax.nn.sigmoid(as_ref[...].astype(F32)) * _dot(sb_ref[...], w_sb_ref[...]))
    y = _dot(mixed.astype(BF16), w_out_ref[...])
    h_res = x_ref[...] + gt1 * _rms(y, g_mix_ref[...])

    h2 = (_rms(h_res, g_pre_ref[...]) * (1.0 + sc2) + sh2).astype(BF16)
    f = None
    fc = 1024
    for c in range(D_FF // fc):
        u = jnp.maximum(_dot(h2, w_ff1_ref[:, c * fc:(c + 1) * fc]), 0.0)
        part = _dot((u * u).astype(BF16), w_ff2_ref[c * fc:(c + 1) * fc, :])
        f = part if f is None else f + part
    o_ref[...] = h_res + gt2 * _rms(f, g_post_ref[...])


def _post(x2, mod3, ret, sb, a_r, a_s, g_mix, g_pre, g_post, w_ret, w_sb, w_out, w_ff1, w_ff2, seq):
    t, d = x2.shape
    tm = TM_POST
    tiles_per_seq = seq // tm
    row = lambda i: (i, 0)
    return pl.pallas_call(
        _post_kernel,
        out_shape=jax.ShapeDtypeStruct((t, d), F32),
        grid=(t // tm,),
        in_specs=[pl.BlockSpec((tm, d), row),
                  pl.BlockSpec((1, 6, d), lambda i: (i // tiles_per_seq, 0, 0)),
                  pl.BlockSpec((tm, RET_V), row),
                  pl.BlockSpec((tm, SB_W), row),
                  pl.BlockSpec((tm, d), row),
                  pl.BlockSpec((tm, d), row),
                  _resident((1, d)), _resident((1, d)), _resident((1, d)),
                  _resident((RET_V, d)), _resident((SB_W, d)), _resident((d, d)),
                  _resident((d, D_FF)), _resident((D_FF, d))],
        out_specs=pl.BlockSpec((tm, d), row),
        compiler_params=pltpu.CompilerParams(dimension_semantics=("arbitrary",),
                                             vmem_limit_bytes=VMEM_LIMIT),
        name="post",
    )(x2, mod3, ret, sb, a_r, a_s, g_mix, g_pre, g_post, w_ret, w_sb, w_out, w_ff1, w_ff2)


def kernel(x, c, positions, ada_w, ada_b, pre_mix_g, post_mix_g, pre_ffn_g, post_ffn_g,
           w_in, ret_gn_g, w_ret_branch, w_sb_branch, w_out, w_ff1, w_ff2):
    bsz, seq, d = x.shape
    depth = ada_w.shape[0]
    assert d == D_MODEL and seq % TM_IN == 0 and seq % TM_POST == 0
    assert seq % RET_L == 0 and seq % SB_T == 0 and RET_L % CHUNK == 0
    t = bsz * seq

    inv_freq = ROPE_BASE ** (-jnp.arange(0, RET_DQK, 2, dtype=F32) / RET_DQK)
    freq = jnp.tile(inv_freq, LANES // (RET_DQK // 2)).reshape(1, LANES)
    sgn = jnp.tile(jnp.concatenate([-jnp.ones(RET_DQK // 2, F32), jnp.ones(RET_DQK // 2, F32)]),
                   LANES // RET_DQK).reshape(1, LANES)
    log_gamma = jnp.log1p(-(2.0 ** (-5.0 - jnp.arange(RET_HEADS, dtype=F32))))
    pos2 = positions.reshape(t, 1)

    h_res = x.astype(F32).reshape(t, d)
    for l in range(depth):
        mod3 = _ada(c.astype(F32), ada_w[l], ada_b[l]).reshape(bsz, 6, d)
        q_r, k_r, v_r, g_r, q_s, k_s, v_s, a_r, a_s = _inproj(
            h_res, mod3, pos2, pre_mix_g[l].reshape(1, d), freq, sgn, w_in[l].astype(BF16), seq)
        ret = _retention(log_gamma, q_r, k_r, v_r, g_r, ret_gn_g[l].reshape(1, RET_V), bsz, seq)
        sb = _stick_breaking(q_s, k_s, v_s, bsz, seq)
        h_res = _post(h_res, mod3, ret, sb, a_r, a_s,
                      post_mix_g[l].reshape(1, d), pre_ffn_g[l].reshape(1, d),
                      post_ffn_g[l].reshape(1, d),
                      w_ret_branch[l].astype(BF16), w_sb_branch[l].astype(BF16),
                      w_out[l].astype(BF16), w_ff1[l].astype(BF16), w_ff2[l].astype(BF16), seq)
    return h_res.reshape(bsz, seq, d).astype(x.dtype)
```

```python
import functools

import jax
import jax.numpy as jnp
import numpy as np
from jax import lax
from jax.experimental import pallas as pl
from jax.experimental.pallas import tpu as pltpu

D_MODEL = 1024
CHUNK = 64
RET_HEADS = 8
RET_DQK = 64
RET_DV = 128
SB_HEADS = 8
SB_DH = 64
D_FF = 4 * D_MODEL
ROPE_BASE = 10000.0
EPS = 1e-6

RET_QK = RET_HEADS * RET_DQK
RET_V = RET_HEADS * RET_DV
SB_W = SB_HEADS * SB_DH
D_IN = 2 * RET_QK + 2 * RET_V + 3 * SB_W + 2 * D_MODEL

LANES = 128
VMEM_LIMIT = 56 * 1024 * 1024

TM_IN = 512
TM_POST = 512
RET_L = 256
SB_T = 256

F32 = jnp.float32
BF16 = jnp.bfloat16


def _dot(a, b):
    return jnp.dot(a, b, preferred_element_type=F32)


def _dot_nt(a, b):
    return lax.dot_general(a, b, (((1,), (1,)), ((), ())), preferred_element_type=F32)


def _dot_tn(a, b):
    return lax.dot_general(a, b, (((0,), (0,)), ((), ())), preferred_element_type=F32)


def _rms(x, g):
    return x * lax.rsqrt(jnp.mean(x * x, axis=-1, keepdims=True) + EPS) * g


def _resident(shape):
    nd = len(shape)
    return pl.BlockSpec(shape, lambda *_: (0,) * nd, pipeline_mode=pl.Buffered(1))


def _ada_kernel(c_ref, w_ref, b_ref, o_ref):
    c = c_ref[...]
    s = (c * jax.nn.sigmoid(c)).astype(BF16)
    o_ref[...] = _dot(s, w_ref[...].astype(BF16)) + b_ref[...]


def _ada(c, w, b):
    bsz, d = c.shape
    n = w.shape[1]
    tn = 1024
    return pl.pallas_call(
        _ada_kernel,
        out_shape=jax.ShapeDtypeStruct((bsz, n), F32),
        grid=(n // tn,),
        in_specs=[pl.BlockSpec((bsz, d), lambda j: (0, 0)),
                  pl.BlockSpec((d, tn), lambda j: (0, j)),
                  pl.BlockSpec((1, tn), lambda j: (0, j))],
        out_specs=pl.BlockSpec((bsz, tn), lambda j: (0, j)),
        compiler_params=pltpu.CompilerParams(dimension_semantics=("arbitrary",)),
        name="ada",
    )(c, w, b.reshape(1, n))


def _inproj_kernel(x_ref, mod_ref, pos_ref, g_ref, freq_ref, sgn_ref, w_ref,
                   qr_ref, kr_ref, vr_ref, gr_ref, qs_ref, ks_ref, vs_ref, ar_ref, as_ref):
    h = _rms(x_ref[...], g_ref[...])
    h = h * (1.0 + mod_ref[0, 1:2, :]) + mod_ref[0, 0:1, :]
    hb = h.astype(BF16)

    ang = pos_ref[...].astype(F32) * freq_ref[...]
    cos = jnp.cos(ang)
    sin = jnp.sin(ang) * sgn_ref[...]
    lane = lax.broadcasted_iota(jnp.int32, (1, LANES), 1)
    first_half = (lane % RET_DQK) < (RET_DQK // 2)

    def rope(p):
        rot = jnp.where(first_half, pltpu.roll(p, LANES - RET_DQK // 2, 1),
                        pltpu.roll(p, RET_DQK // 2, 1))
        return p * cos + rot * sin

    def proj(c0, width):
        return _dot(hb, w_ref[:, c0:c0 + width])

    c0 = 0
    acc = proj(c0, RET_QK)
    for j in range(RET_QK // LANES):
        qr_ref[:, j * LANES:(j + 1) * LANES] = rope(acc[:, j * LANES:(j + 1) * LANES]).astype(BF16)
    c0 += RET_QK
    acc = proj(c0, RET_QK)
    kscale = RET_DQK ** -0.5
    for j in range(RET_QK // LANES):
        kr_ref[:, j * LANES:(j + 1) * LANES] = (
            rope(acc[:, j * LANES:(j + 1) * LANES]) * kscale).astype(BF16)
    c0 += RET_QK
    for ref, width in ((vr_ref, RET_V), (gr_ref, RET_V), (qs_ref, SB_W), (ks_ref, SB_W),
                       (vs_ref, SB_W), (ar_ref, D_MODEL), (as_ref, D_MODEL)):
        ref[...] = proj(c0, width).astype(BF16)
        c0 += width


def _inproj(x2, mod3, pos2, g, freq, sgn, w_bf, seq):
    t, d = x2.shape
    tm = TM_IN
    tiles_per_seq = seq // tm
    widths = (RET_QK, RET_QK, RET_V, RET_V, SB_W, SB_W, SB_W, D_MODEL, D_MODEL)
    row = lambda i: (i, 0)
    return pl.pallas_call(
        _inproj_kernel,
        out_shape=[jax.ShapeDtypeStruct((t, w), BF16) for w in widths],
        grid=(t // tm,),
        in_specs=[pl.BlockSpec((tm, d), row),
                  pl.BlockSpec((1, 6, d), lambda i: (i // tiles_per_seq, 0, 0)),
                  pl.BlockSpec((tm, 1), row),
                  _resident((1, d)),
                  _resident((1, LANES)),
                  _resident((1, LANES)),
                  _resident((d, D_IN))],
        out_specs=[pl.BlockSpec((tm, w), row) for w in widths],
        compiler_params=pltpu.CompilerParams(dimension_semantics=("arbitrary",),
                                             vmem_limit_bytes=VMEM_LIMIT),
        name="inproj",
    )(x2, mod3, pos2, g, freq, sgn, w_bf)


def _ret_kernel(lg_ref, q_ref, k_ref, v_ref, g_ref, gn_ref, o_ref, state_ref, dtab_ref):
    b = pl.program_id(0)
    j = pl.program_id(1)
    blk = RET_L

    @pl.when(jnp.logical_and(b == 0, j == 0))
    def _():
        ri = lax.broadcasted_iota(jnp.int32, (blk, blk), 0)
        ci = lax.broadcasted_iota(jnp.int32, (blk, blk), 1)
        dist = jnp.abs(ri - ci).astype(F32)
        keep = (ci // CHUNK) <= (ri // CHUNK)
        for h in range(RET_HEADS):
            dtab_ref[h] = jnp.where(keep, jnp.exp(dist * lg_ref[h]), 0.0)

    @pl.when(j == 0)
    def _():
        state_ref[...] = jnp.zeros_like(state_ref)

    lane = lax.broadcasted_iota(jnp.int32, (1, LANES), 1)
    idx = lax.broadcasted_iota(jnp.int32, (blk, LANES), 0).astype(F32)
    for h in range(RET_HEADS):
        p = h // 2
        lg = lg_ref[h]
        in_head = (lane // RET_DQK) == (h % 2)
        qp = q_ref[:, p * LANES:(p + 1) * LANES]
        kp = k_ref[:, p * LANES:(p + 1) * LANES]
        vh = v_ref[:, h * RET_DV:(h + 1) * RET_DV]
        qh = jnp.where(in_head, qp, jnp.zeros_like(qp))
        st = state_ref[h]

        scores = _dot_nt(qh, kp) * dtab_ref[h]
        intra = _dot(scores.astype(BF16), vh)
        cross = _dot(qh, st.astype(BF16)) * jnp.exp((idx + 1.0) * lg)
        kd = (kp.astype(F32) * jnp.exp((blk - 1.0 - idx) * lg)).astype(BF16)
        state_ref[h] = st * jnp.exp(blk * lg) + _dot_tn(kd, vh)

        o = intra + cross
        mu = jnp.mean(o, axis=-1, keepdims=True)
        dlt = o - mu
        var = jnp.mean(dlt * dlt, axis=-1, keepdims=True)
        y = dlt * lax.rsqrt(var + EPS) * gn_ref[:, h * RET_DV:(h + 1) * RET_DV]
        g = g_ref[:, h * RET_DV:(h + 1) * RET_DV].astype(F32)
        o_ref[:, h * RET_DV:(h + 1) * RET_DV] = (g * jax.nn.sigmoid(g) * y).astype(BF16)


def _retention(log_gamma, q, k, v, g, gn, bsz, seq):
    t = q.shape[0]
    nb = seq // RET_L
    row = lambda b, j: (b * nb + j, 0)
    return pl.pallas_call(
        _ret_kernel,
        out_shape=jax.ShapeDtypeStruct((t, RET_V), BF16),
        grid=(bsz, nb),
        in_specs=[pl.BlockSpec(memory_space=pltpu.SMEM),
                  pl.BlockSpec((RET_L, RET_QK), row),
                  pl.BlockSpec((RET_L, RET_QK), row),
                  pl.BlockSpec((RET_L, RET_V), row),
                  pl.BlockSpec((RET_L, RET_V), row),
                  pl.BlockSpec((1, RET_V), lambda b, j: (0, 0))],
        out_specs=pl.BlockSpec((RET_L, RET_V), row),
        scratch_shapes=[pltpu.VMEM((RET_HEADS, LANES, RET_DV), F32),
                        pltpu.VMEM((RET_HEADS, RET_L, RET_L), F32)],
        compiler_params=pltpu.CompilerParams(dimension_semantics=("arbitrary", "arbitrary"),
                                             vmem_limit_bytes=VMEM_LIMIT),
        name="retention",
    )(log_gamma, q, k, v, g, gn)


def _sb_kernel(q_ref, k_ref, v_ref, o_ref, qm_ref, tri_ref, acc_ref, r_ref):
    qi = pl.program_id(1)
    t = SB_T
    lane = lax.broadcasted_iota(jnp.int32, (1, LANES), 1)
    ri = lax.broadcasted_iota(jnp.int32, (t, t), 0)
    ci = lax.broadcasted_iota(jnp.int32, (t, t), 1)

    @pl.when(jnp.logical_and(pl.program_id(0) == 0, qi == 0))
    def _():
        tri_ref[...] = jnp.where(ri > ci, 1.0, 0.0).astype(BF16)

    for h in range(SB_HEADS):
        qp = q_ref[:, (h // 2) * LANES:(h // 2 + 1) * LANES]
        in_head = (lane // SB_DH) == (h % 2)
        qm_ref[h] = jnp.where(in_head, qp, jnp.zeros_like(qp)) * (SB_DH ** -0.5)

    def block(h, start, diag):
        cols = slice((h // 2) * LANES, (h // 2 + 1) * LANES)
        z = _dot_nt(qm_ref[h], k_ref[pl.ds(start, t), cols])
        neg_abs = pltpu.bitcast(pltpu.bitcast(z, jnp.uint32) | jnp.uint32(0x80000000), F32)
        sp = jnp.maximum(z, 0.0) + jnp.log(1.0 + jnp.exp(neg_abs))
        if diag:
            valid = ci < ri
            sp = jnp.where(valid, sp, 0.0)
        later = _dot(sp.astype(BF16), tri_ref[...])
        arg = (z - sp) - later
        if not diag:
            r = r_ref[h]
            arg = arg - jnp.concatenate([r] * (t // LANES), axis=-1)
        a = jnp.exp(arg)
        if diag:
            a = jnp.where(valid, a, 0.0)
        pv = _dot(a.astype(BF16), v_ref[pl.ds(start, t), cols])
        rs = jnp.broadcast_to(jnp.sum(sp, axis=-1, keepdims=True), (t, LANES))
        if diag:
            acc_ref[h] = pv
            r_ref[h] = rs
        else:
            acc_ref[h] += pv
            r_ref[h] += rs

    for h in range(SB_HEADS):
        block(h, pl.multiple_of(qi * t, t), True)

    def body(it, carry):
        start = pl.multiple_of((qi - 1 - it) * t, t)
        for h in range(SB_HEADS):
            block(h, start, False)
        return carry

    lax.fori_loop(0, qi, body, 0)

    for p in range(SB_HEADS // 2):
        o_ref[:, p * LANES:(p + 1) * LANES] = jnp.where(
            lane < SB_DH, acc_ref[2 * p], acc_ref[2 * p + 1]).astype(BF16)


def _stick_breaking(q, k, v, bsz, seq):
    t = q.shape[0]
    nq = seq // SB_T
    return pl.pallas_call(
        _sb_kernel,
        out_shape=jax.ShapeDtypeStruct((t, SB_W), BF16),
        grid=(bsz, nq),
        in_specs=[pl.BlockSpec((SB_T, SB_W), lambda b, i: (b * nq + i, 0)),
                  pl.BlockSpec((seq, SB_W), lambda b, i: (b, 0)),
                  pl.BlockSpec((seq, SB_W), lambda b, i: (b, 0))],
        out_specs=pl.BlockSpec((SB_T, SB_W), lambda b, i: (b * nq + i, 0)),
        scratch_shapes=[pltpu.VMEM((SB_HEADS, SB_T, LANES), BF16),
                        pltpu.VMEM((SB_T, SB_T), BF16),
                        pltpu.VMEM((SB_HEADS, SB_T, LANES), F32),
                        pltpu.VMEM((SB_HEADS, SB_T, LANES), F32)],
        compiler_params=pltpu.CompilerParams(dimension_semantics=("arbitrary", "arbitrary"),
                                             vmem_limit_bytes=VMEM_LIMIT),
        name="stickbreak",
    )(q, k, v)


def _post_kernel(x_ref, mod_ref, ret_ref, sb_ref, ar_ref, as_ref,
                 g_mix_ref, g_pre_ref, g_post_ref,
                 w_ret_ref, w_sb_ref, w_out_ref, w_ff1_ref, w_ff2_ref, o_ref):
    gt1 = mod_ref[0, 2:3, :]
    sh2 = mod_ref[0, 3:4, :]
    sc2 = mod_ref[0, 4:5, :]
    gt2 = mod_ref[0, 5:6, :]

    mixed = (jax.nn.sigmoid(ar_ref[...].astype(F32)) * _dot(ret_ref[...], w_ret_ref[...])
             + jax.nn.sigmoid(as_ref[...].astype(F32)) * _dot(sb_ref[...], w_sb_ref[...]))
    y = _dot(mixed.astype(BF16), w_out_ref[...])
    h_res = x_ref[...] + gt1 * _rms(y, g_mix_ref[...])

    h2 = (_rms(h_res, g_pre_ref[...]) * (1.0 + sc2) + sh2).astype(BF16)
    f = None
    fc = 1024
    for c in range(D_FF // fc):
        u = jnp.maximum(_dot(h2, w_ff1_ref[:, c * fc:(c + 1) * fc]), 0.0)
        part = _dot((u * u).astype(BF16), w_ff2_ref[c * fc:(c + 1) * fc, :])
        f = part if f is None else f + part
    o_ref[...] = h_res + gt2 * _rms(f, g_post_ref[...])


def _post(x2, mod3, ret, sb, a_r, a_s, g_mix, g_pre, g_post, w_ret, w_sb, w_out, w_ff1, w_ff2, seq):
    t, d = x2.shape
    tm = TM_POST
    tiles_per_seq = seq // tm
    row = lambda i: (i, 0)
    return pl.pallas_call(
        _post_kernel,
        out_shape=jax.ShapeDtypeStruct((t, d), F32),
        grid=(t // tm,),
        in_specs=[pl.BlockSpec((tm, d), row),
                  pl.BlockSpec((1, 6, d), lambda i: (i // tiles_per_seq, 0, 0)),
                  pl.BlockSpec((tm, RET_V), row),
                  pl.BlockSpec((tm, SB_W), row),
                  pl.BlockSpec((tm, d), row),
                  pl.BlockSpec((tm, d), row),
                  _resident((1, d)), _resident((1, d)), _resident((1, d)),
                  _resident((RET_V, d)), _resident((SB_W, d)), _resident((d, d)),
                  _resident((d, D_FF)), _resident((D_FF, d))],
        out_specs=pl.BlockSpec((tm, d), row),
        compiler_params=pltpu.CompilerParams(dimension_semantics=("arbitrary",),
                                             vmem_limit_bytes=VMEM_LIMIT),
        name="post",
    )(x2, mod3, ret, sb, a_r, a_s, g_mix, g_pre, g_post, w_ret, w_sb, w_out, w_ff1, w_ff2)


def kernel(x, c, positions, ada_w, ada_b, pre_mix_g, post_mix_g, pre_ffn_g, post_ffn_g,
           w_in, ret_gn_g, w_ret_branch, w_sb_branch, w_out, w_ff1, w_ff2):
    bsz, seq, d = x.shape
    depth = ada_w.shape[0]
    assert d == D_MODEL and seq % TM_IN == 0 and seq % TM_POST == 0
    assert seq % RET_L == 0 and seq % SB_T == 0 and RET_L % CHUNK == 0
    t = bsz * seq

    inv_freq = ROPE_BASE ** (-jnp.arange(0, RET_DQK, 2, dtype=F32) / RET_DQK)
    freq = jnp.tile(inv_freq, LANES // (RET_DQK // 2)).reshape(1, LANES)
    sgn = jnp.tile(jnp.concatenate([-jnp.ones(RET_DQK // 2, F32), jnp.ones(RET_DQK // 2, F32)]),
                   LANES // RET_DQK).reshape(1, LANES)
    log_gamma = jnp.log1p(-(2.0 ** (-5.0 - jnp.arange(RET_HEADS, dtype=F32))))
    pos2 = positions.reshape(t, 1)

    h_res = x.astype(F32).reshape(t, d)
    for l in range(depth):
        mod3 = _ada(c.astype(F32), ada_w[l], ada_b[l]).reshape(bsz, 6, d)
        q_r, k_r, v_r, g_r, q_s, k_s, v_s, a_r, a_s = _inproj(
            h_res, mod3, pos2, pre_mix_g[l].reshape(1, d), freq, sgn, w_in[l].astype(BF16), seq)
        ret = _retention(log_gamma, q_r, k_r, v_r, g_r, ret_gn_g[l].reshape(1, RET_V), bsz, seq)
        sb = _stick_breaking(q_s, k_s, v_s, bsz, seq)
        h_res = _post(h_res, mod3, ret, sb, a_r, a_s,
                      post_mix_g[l].reshape(1, d), pre_ffn_g[l].reshape(1, d),
                      post_ffn_g[l].reshape(1, d),
                      w_ret_branch[l].astype(BF16), w_sb_branch[l].astype(BF16),
                      w_out[l].astype(BF16), w_ff1[l].astype(BF16), w_ff2[l].astype(BF16), seq)
    return h_res.reshape(bsz, seq, d).astype(x.dtype)
```

```python
import functools

import jax
import jax.numpy as jnp
import numpy as np
from jax import lax
from jax.experimental import pallas as pl
from jax.experimental.pallas import tpu as pltpu

D_MODEL = 1024
CHUNK = 64
RET_HEADS = 8
RET_DQK = 64
RET_DV = 128
SB_HEADS = 8
SB_DH = 64
D_FF = 4 * D_MODEL
ROPE_BASE = 10000.0
EPS = 1e-6

RET_QK = RET_HEADS * RET_DQK
RET_V = RET_HEADS * RET_DV
SB_W = SB_HEADS * SB_DH
D_IN = 2 * RET_QK + 2 * RET_V + 3 * SB_W + 2 * D_MODEL

LANES = 128
VMEM_LIMIT = 56 * 1024 * 1024

TM_IN = 512
TM_POST = 512
RET_L = 256
SB_T = 256
SB_SKIP_LOG = 105.0
LOG2E = 1.4426950408889634

F32 = jnp.float32
BF16 = jnp.bfloat16


def _dot(a, b):
    return jnp.dot(a, b, preferred_element_type=F32)


def _dot_nt(a, b):
    return lax.dot_general(a, b, (((1,), (1,)), ((), ())), preferred_element_type=F32)


def _dot_tn(a, b):
    return lax.dot_general(a, b, (((0,), (0,)), ((), ())), preferred_element_type=F32)


def _rms(x, g):
    return x * lax.rsqrt(jnp.mean(x * x, axis=-1, keepdims=True) + EPS) * g


def _resident(shape):
    nd = len(shape)
    return pl.BlockSpec(shape, lambda *_: (0,) * nd, pipeline_mode=pl.Buffered(1))


def _ada_kernel(c_ref, w_ref, b_ref, o_ref):
    c = c_ref[...]
    s = (c * jax.nn.sigmoid(c)).astype(BF16)
    o_ref[...] = _dot(s, w_ref[...].astype(BF16)) + b_ref[...]


def _ada(c, w, b):
    bsz, d = c.shape
    n = w.shape[1]
    tn = 1024
    return pl.pallas_call(
        _ada_kernel,
        out_shape=jax.ShapeDtypeStruct((bsz, n), F32),
        grid=(n // tn,),
        in_specs=[pl.BlockSpec((bsz, d), lambda j: (0, 0)),
                  pl.BlockSpec((d, tn), lambda j: (0, j)),
                  pl.BlockSpec((1, tn), lambda j: (0, j))],
        out_specs=pl.BlockSpec((bsz, tn), lambda j: (0, j)),
        compiler_params=pltpu.CompilerParams(dimension_semantics=("arbitrary",)),
        name="ada",
    )(c, w, b.reshape(1, n))


def _inproj_kernel(x_ref, mod_ref, pos_ref, g_ref, freq_ref, sgn_ref, w_ref,
                   qr_ref, kr_ref, vr_ref, gr_ref, qs_ref, ks_ref, vs_ref, ar_ref, as_ref):
    h = _rms(x_ref[...], g_ref[...])
    h = h * (1.0 + mod_ref[0, 1:2, :]) + mod_ref[0, 0:1, :]
    hb = h.astype(BF16)

    ang = pos_ref[...].astype(F32) * freq_ref[...]
    cos = jnp.cos(ang)
    sin = jnp.sin(ang) * sgn_ref[...]
    lane = lax.broadcasted_iota(jnp.int32, (1, LANES), 1)
    first_half = (lane % RET_DQK) < (RET_DQK // 2)

    def rope(p):
        rot = jnp.where(first_half, pltpu.roll(p, LANES - RET_DQK // 2, 1),
                        pltpu.roll(p, RET_DQK // 2, 1))
        return p * cos + rot * sin

    def proj(c0, width):
        return _dot(hb, w_ref[:, c0:c0 + width])

    c0 = 0
    acc = proj(c0, RET_QK)
    for j in range(RET_QK // LANES):
        qr_ref[:, j * LANES:(j + 1) * LANES] = rope(acc[:, j * LANES:(j + 1) * LANES]).astype(BF16)
    c0 += RET_QK
    acc = proj(c0, RET_QK)
    kscale = RET_DQK ** -0.5
    for j in range(RET_QK // LANES):
        kr_ref[:, j * LANES:(j + 1) * LANES] = (
            rope(acc[:, j * LANES:(j + 1) * LANES]) * kscale).astype(BF16)
    c0 += RET_QK
    for ref, width in ((vr_ref, RET_V), (gr_ref, RET_V), (qs_ref, SB_W), (ks_ref, SB_W),
                       (vs_ref, SB_W), (ar_ref, D_MODEL), (as_ref, D_MODEL)):
        ref[...] = proj(c0, width).astype(BF16)
        c0 += width


def _inproj(x2, mod3, pos2, g, freq, sgn, w_bf, seq):
    t, d = x2.shape
    tm = TM_IN
    tiles_per_seq = seq // tm
    widths = (RET_QK, RET_QK, RET_V, RET_V, SB_W, SB_W, SB_W, D_MODEL, D_MODEL)
    row = lambda i: (i, 0)
    return pl.pallas_call(
        _inproj_kernel,
        out_shape=[jax.ShapeDtypeStruct((t, w), BF16) for w in widths],
        grid=(t // tm,),
        in_specs=[pl.BlockSpec((tm, d), row),
                  pl.BlockSpec((1, 6, d), lambda i: (i // tiles_per_seq, 0, 0)),
                  pl.BlockSpec((tm, 1), row),
                  _resident((1, d)),
                  _resident((1, LANES)),
                  _resident((1, LANES)),
                  _resident((d, D_IN))],
        out_specs=[pl.BlockSpec((tm, w), row) for w in widths],
        compiler_params=pltpu.CompilerParams(dimension_semantics=("arbitrary",),
                                             vmem_limit_bytes=VMEM_LIMIT),
        name="inproj",
    )(x2, mod3, pos2, g, freq, sgn, w_bf)


def _ret_kernel(lg_ref, q_ref, k_ref, v_ref, g_ref, gn_ref, o_ref, state_ref, dtab_ref):
    b = pl.program_id(0)
    j = pl.program_id(1)
    blk = RET_L

    @pl.when(jnp.logical_and(b == 0, j == 0))
    def _():
        ri = lax.broadcasted_iota(jnp.int32, (blk, blk), 0)
        ci = lax.broadcasted_iota(jnp.int32, (blk, blk), 1)
        dist = jnp.abs(ri - ci).astype(F32)
        keep = (ci // CHUNK) <= (ri // CHUNK)
        for h in range(RET_HEADS):
            dtab_ref[h] = jnp.where(keep, jnp.exp(dist * lg_ref[h]), 0.0)

    @pl.when(j == 0)
    def _():
        state_ref[...] = jnp.zeros_like(state_ref)

    lane = lax.broadcasted_iota(jnp.int32, (1, LANES), 1)
    idx = lax.broadcasted_iota(jnp.int32, (blk, LANES), 0).astype(F32)
    for h in range(RET_HEADS):
        p = h // 2
        lg = lg_ref[h]
        in_head = (lane // RET_DQK) == (h % 2)
        qp = q_ref[:, p * LANES:(p + 1) * LANES]
        kp = k_ref[:, p * LANES:(p + 1) * LANES]
        vh = v_ref[:, h * RET_DV:(h + 1) * RET_DV]
        qh = jnp.where(in_head, qp, jnp.zeros_like(qp))
        st = state_ref[h]

        scores = _dot_nt(qh, kp) * dtab_ref[h]
        intra = _dot(scores.astype(BF16), vh)
        cross = _dot(qh, st.astype(BF16)) * jnp.exp((idx + 1.0) * lg)
        kd = (kp.astype(F32) * jnp.exp((blk - 1.0 - idx) * lg)).astype(BF16)
        state_ref[h] = st * jnp.exp(blk * lg) + _dot_tn(kd, vh)

        o = intra + cross
        mu = jnp.mean(o, axis=-1, keepdims=True)
        dlt = o - mu
        var = jnp.mean(dlt * dlt, axis=-1, keepdims=True)
        y = dlt * lax.rsqrt(var + EPS) * gn_ref[:, h * RET_DV:(h + 1) * RET_DV]
        g = g_ref[:, h * RET_DV:(h + 1) * RET_DV].astype(F32)
        o_ref[:, h * RET_DV:(h + 1) * RET_DV] = (g * jax.nn.sigmoid(g) * y).astype(BF16)


def _retention(log_gamma, q, k, v, g, gn, bsz, seq):
    t = q.shape[0]
    nb = seq // RET_L
    row = lambda b, j: (b * nb + j, 0)
    return pl.pallas_call(
        _ret_kernel,
        out_shape=jax.ShapeDtypeStruct((t, RET_V), BF16),
        grid=(bsz, nb),
        in_specs=[pl.BlockSpec(memory_space=pltpu.SMEM),
                  pl.BlockSpec((RET_L, RET_QK), row),
                  pl.BlockSpec((RET_L, RET_QK), row),
                  pl.BlockSpec((RET_L, RET_V), row),
                  pl.BlockSpec((RET_L, RET_V), row),
                  pl.BlockSpec((1, RET_V), lambda b, j: (0, 0))],
        out_specs=pl.BlockSpec((RET_L, RET_V), row),
        scratch_shapes=[pltpu.VMEM((RET_HEADS, LANES, RET_DV), F32),
                        pltpu.VMEM((RET_HEADS, RET_L, RET_L), F32)],
        compiler_params=pltpu.CompilerParams(dimension_semantics=("arbitrary", "arbitrary"),
                                             vmem_limit_bytes=VMEM_LIMIT),
        name="retention",
    )(log_gamma, q, k, v, g, gn)


def _sb_kernel(q_ref, k_ref, v_ref, o_ref, qm_ref, tri_ref, acc_ref, r_ref):
    qi = pl.program_id(1)
    t = SB_T
    lane = lax.broadcasted_iota(jnp.int32, (1, LANES), 1)
    ri = lax.broadcasted_iota(jnp.int32, (t, t), 0)
    ci = lax.broadcasted_iota(jnp.int32, (t, t), 1)

    @pl.when(jnp.logical_and(pl.program_id(0) == 0, qi == 0))
    def _():
        tri_ref[...] = jnp.where(ri > ci, 1.0, 0.0).astype(BF16)

    for h in range(SB_HEADS):
        qp = q_ref[:, (h // 2) * LANES:(h // 2 + 1) * LANES]
        in_head = (lane // SB_DH) == (h % 2)
        qm_ref[h] = jnp.where(in_head, qp, jnp.zeros_like(qp)) * (SB_DH ** -0.5)

    def block(h, start, diag):
        cols = slice((h // 2) * LANES, (h // 2 + 1) * LANES)
        z = _dot_nt(qm_ref[h], k_ref[pl.ds(start, t), cols])
        sp = jnp.maximum(z, 0.0) + jnp.log(1.0 + jnp.exp2(jnp.abs(z) * (-LOG2E)))
        if diag:
            valid = ci < ri
            sp = jnp.where(valid, sp, 0.0)
        later = _dot(sp.astype(BF16), tri_ref[...])
        arg = (z - sp) - later
        if not diag:
            r = r_ref[h]
            arg = arg - jnp.concatenate([r] * (t // LANES), axis=-1)
        a = jnp.exp(arg)
        if diag:
            a = jnp.where(valid, a, 0.0)
        pv = _dot(a.astype(BF16), v_ref[pl.ds(start, t), cols])
        rs = jnp.broadcast_to(jnp.sum(sp, axis=-1, keepdims=True), (t, LANES))
        if diag:
            acc_ref[h] = pv
            r_ref[h] = rs
        else:
            acc_ref[h] += pv
            r_ref[h] += rs

    for h in range(SB_HEADS):
        block(h, pl.multiple_of(qi * t, t), True)

    def cond(carry):
        it, live = carry
        return jnp.logical_and(it < qi, live)

    def body(carry):
        it, _ = carry
        start = pl.multiple_of((qi - 1 - it) * t, t)
        r_min = None
        for h in range(SB_HEADS):
            block(h, start, False)
            r_min = r_ref[h] if r_min is None else jnp.minimum(r_min, r_ref[h])
        return it + 1, jnp.min(r_min) < SB_SKIP_LOG

    lax.while_loop(cond, body, (jnp.int32(0), True))

    for p in range(SB_HEADS // 2):
        o_ref[:, p * LANES:(p + 1) * LANES] = jnp.where(
            lane < SB_DH, acc_ref[2 * p], acc_ref[2 * p + 1]).astype(BF16)


def _stick_breaking(q, k, v, bsz, seq):
    t = q.shape[0]
    nq = seq // SB_T
    return pl.pallas_call(
        _sb_kernel,
        out_shape=jax.ShapeDtypeStruct((t, SB_W), BF16),
        grid=(bsz, nq),
        in_specs=[pl.BlockSpec((SB_T, SB_W), lambda b, i: (b * nq + i, 0)),
                  pl.BlockSpec((seq, SB_W), lambda b, i: (b, 0)),
                  pl.BlockSpec((seq, SB_W), lambda b, i: (b, 0))],
        out_specs=pl.BlockSpec((SB_T, SB_W), lambda b, i: (b * nq + i, 0)),
        scratch_shapes=[pltpu.VMEM((SB_HEADS, SB_T, LANES), BF16),
                        pltpu.VMEM((SB_T, SB_T), BF16),
                        pltpu.VMEM((SB_HEADS, SB_T, LANES), F32),
                        pltpu.VMEM((SB_HEADS, SB_T, LANES), F32)],
        compiler_params=pltpu.CompilerParams(dimension_semantics=("arbitrary", "arbitrary"),
                                             vmem_limit_bytes=VMEM_LIMIT),
        name="stickbreak",
    )(q, k, v)


def _post_kernel(x_ref, mod_ref, ret_ref, sb_ref, ar_ref, as_ref,
                 g_mix_ref, g_pre_ref, g_post_ref,
                 w_ret_ref, w_sb_ref, w_out_ref, w_ff1_ref, w_ff2_ref, o_ref):
    gt1 = mod_ref[0, 2:3, :]
    sh2 = mod_ref[0, 3:4, :]
    sc2 = mod_ref[0, 4:5, :]
    gt2 = mod_ref[0, 5:6, :]

    mixed = (jax.nn.sigmoid(ar_ref[...].astype(F32)) * _dot(ret_ref[...], w_ret_ref[...])
             + jax.nn.sigmoid(as_ref[...].astype(F32)) * _dot(sb_ref[...], w_sb_ref[...]))
    y = _dot(mixed.astype(BF16), w_out_ref[...])
    h_res = x_ref[...] + gt1 * _rms(y, g_mix_ref[...])

    h2 = (_rms(h_res, g_pre_ref[...]) * (1.0 + sc2) + sh2).astype(BF16)
    f = None
    fc = 1024
    for c in range(D_FF // fc):
        u = jnp.maximum(_dot(h2, w_ff1_ref[:, c * fc:(c + 1) * fc]), 0.0)
        part = _dot((u * u).astype(BF16), w_ff2_ref[c * fc:(c + 1) * fc, :])
        f = part if f is None else f + part
    o_ref[...] = h_res + gt2 * _rms(f, g_post_ref[...])


def _post(x2, mod3, ret, sb, a_r, a_s, g_mix, g_pre, g_post, w_ret, w_sb, w_out, w_ff1, w_ff2, seq):
    t, d = x2.shape
    tm = TM_POST
    tiles_per_seq = seq // tm
    row = lambda i: (i, 0)
    return pl.pallas_call(
        _post_kernel,
        out_shape=jax.ShapeDtypeStruct((t, d), F32),
        grid=(t // tm,),
        in_specs=[pl.BlockSpec((tm, d), row),
                  pl.BlockSpec((1, 6, d), lambda i: (i // tiles_per_seq, 0, 0)),
                  pl.BlockSpec((tm, RET_V), row),
                  pl.BlockSpec((tm, SB_W), row),
                  pl.BlockSpec((tm, d), row),
                  pl.BlockSpec((tm, d), row),
                  _resident((1, d)), _resident((1, d)), _resident((1, d)),
                  _resident((RET_V, d)), _resident((SB_W, d)), _resident((d, d)),
                  _resident((d, D_FF)), _resident((D_FF, d))],
        out_specs=pl.BlockSpec((tm, d), row),
        compiler_params=pltpu.CompilerParams(dimension_semantics=("arbitrary",),
                                             vmem_limit_bytes=VMEM_LIMIT),
        name="post",
    )(x2, mod3, ret, sb, a_r, a_s, g_mix, g_pre, g_post, w_ret, w_sb, w_out, w_ff1, w_ff2)


def kernel(x, c, positions, ada_w, ada_b, pre_mix_g, post_mix_g, pre_ffn_g, post_ffn_g,
           w_in, ret_gn_g, w_ret_branch, w_sb_branch, w_out, w_ff1, w_ff2):
    bsz, seq, d = x.shape
    depth = ada_w.shape[0]
    assert d == D_MODEL and seq % TM_IN == 0 and seq % TM_POST == 0
    assert seq % RET_L == 0 and seq % SB_T == 0 and RET_L % CHUNK == 0
    t = bsz * seq

    inv_freq = ROPE_BASE ** (-jnp.arange(0, RET_DQK, 2, dtype=F32) / RET_DQK)
    freq = jnp.tile(inv_freq, LANES // (RET_DQK // 2)).reshape(1, LANES)
    sgn = jnp.tile(jnp.concatenate([-jnp.ones(RET_DQK // 2, F32), jnp.ones(RET_DQK // 2, F32)]),
                   LANES // RET_DQK).reshape(1, LANES)
    log_gamma = jnp.log1p(-(2.0 ** (-5.0 - jnp.arange(RET_HEADS, dtype=F32))))
    pos2 = positions.reshape(t, 1)

    h_res = x.astype(F32).reshape(t, d)
    for l in range(depth):
        mod3 = _ada(c.astype(F32), ada_w[l], ada_b[l]).reshape(bsz, 6, d)
        q_r, k_r, v_r, g_r, q_s, k_s, v_s, a_r, a_s = _inproj(
            h_res, mod3, pos2, pre_mix_g[l].reshape(1, d), freq, sgn, w_in[l].astype(BF16), seq)
        ret = _retention(log_gamma, q_r, k_r, v_r, g_r, ret_gn_g[l].reshape(1, RET_V), bsz, seq)
        sb = _stick_breaking(q_s, k_s, v_s, bsz, seq)
        h_res = _post(h_res, mod3, ret, sb, a_r, a_s,
                      post_mix_g[l].reshape(1, d), pre_ffn_g[l].reshape(1, d),
                      post_ffn_g[l].reshape(1, d),
                      w_ret_branch[l].astype(BF16), w_sb_branch[l].astype(BF16),
                      w_out[l].astype(BF16), w_ff1[l].astype(BF16), w_ff2[l].astype(BF16), seq)
    return h_res.reshape(bsz, seq, d).astype(x.dtype)
```

```python
import functools

import jax
import jax.numpy as jnp
import numpy as np
from jax import lax
from jax.experimental import pallas as pl
from jax.experimental.pallas import tpu as pltpu

D_MODEL = 1024
CHUNK = 64
RET_HEADS = 8
RET_DQK = 64
RET_DV = 128
SB_HEADS = 8
SB_DH = 64
D_FF = 4 * D_MODEL
ROPE_BASE = 10000.0
EPS = 1e-6

RET_QK = RET_HEADS * RET_DQK
RET_V = RET_HEADS * RET_DV
SB_W = SB_HEADS * SB_DH
D_IN = 2 * RET_QK + 2 * RET_V + 3 * SB_W + 2 * D_MODEL

LANES = 128
VMEM_LIMIT = 56 * 1024 * 1024

TM_IN = 512
TM_POST = 512
RET_L = 256
SB_T = 256
SB_SKIP_LOG = 105.0
SB_SKEW = 1
LOG2E = 1.4426950408889634

F32 = jnp.float32
BF16 = jnp.bfloat16


def _dot(a, b):
    return jnp.dot(a, b, preferred_element_type=F32)


def _dot_nt(a, b):
    return lax.dot_general(a, b, (((1,), (1,)), ((), ())), preferred_element_type=F32)


def _dot_tn(a, b):
    return lax.dot_general(a, b, (((0,), (0,)), ((), ())), preferred_element_type=F32)


def _rms(x, g):
    return x * lax.rsqrt(jnp.mean(x * x, axis=-1, keepdims=True) + EPS) * g


def _resident(shape):
    nd = len(shape)
    return pl.BlockSpec(shape, lambda *_: (0,) * nd, pipeline_mode=pl.Buffered(1))


def _ada_kernel(c_ref, w_ref, b_ref, o_ref):
    c = c_ref[...]
    s = (c * jax.nn.sigmoid(c)).astype(BF16)
    o_ref[...] = _dot(s, w_ref[...].astype(BF16)) + b_ref[...]


def _ada(c, w, b):
    bsz, d = c.shape
    n = w.shape[1]
    tn = 1024
    return pl.pallas_call(
        _ada_kernel,
        out_shape=jax.ShapeDtypeStruct((bsz, n), F32),
        grid=(n // tn,),
        in_specs=[pl.BlockSpec((bsz, d), lambda j: (0, 0)),
                  pl.BlockSpec((d, tn), lambda j: (0, j)),
                  pl.BlockSpec((1, tn), lambda j: (0, j))],
        out_specs=pl.BlockSpec((bsz, tn), lambda j: (0, j)),
        compiler_params=pltpu.CompilerParams(dimension_semantics=("arbitrary",)),
        name="ada",
    )(c, w, b.reshape(1, n))


def _inproj_kernel(x_ref, mod_ref, pos_ref, g_ref, freq_ref, sgn_ref, w_ref,
                   qr_ref, kr_ref, vr_ref, gr_ref, qs_ref, ks_ref, vs_ref, ar_ref, as_ref):
    h = _rms(x_ref[...], g_ref[...])
    h = h * (1.0 + mod_ref[0, 1:2, :]) + mod_ref[0, 0:1, :]
    hb = h.astype(BF16)

    ang = pos_ref[...].astype(F32) * freq_ref[...]
    cos = jnp.cos(ang)
    sin = jnp.sin(ang) * sgn_ref[...]
    lane = lax.broadcasted_iota(jnp.int32, (1, LANES), 1)
    first_half = (lane % RET_DQK) < (RET_DQK // 2)

    def rope(p):
        rot = jnp.where(first_half, pltpu.roll(p, LANES - RET_DQK // 2, 1),
                        pltpu.roll(p, RET_DQK // 2, 1))
        return p * cos + rot * sin

    def proj(c0, width):
        return _dot(hb, w_ref[:, c0:c0 + width])

    c0 = 0
    acc = proj(c0, RET_QK)
    for j in range(RET_QK // LANES):
        qr_ref[:, j * LANES:(j + 1) * LANES] = rope(acc[:, j * LANES:(j + 1) * LANES]).astype(BF16)
    c0 += RET_QK
    acc = proj(c0, RET_QK)
    kscale = RET_DQK ** -0.5
    for j in range(RET_QK // LANES):
        kr_ref[:, j * LANES:(j + 1) * LANES] = (
            rope(acc[:, j * LANES:(j + 1) * LANES]) * kscale).astype(BF16)
    c0 += RET_QK
    for ref, width in ((vr_ref, RET_V), (gr_ref, RET_V), (qs_ref, SB_W), (ks_ref, SB_W),
                       (vs_ref, SB_W), (ar_ref, D_MODEL), (as_ref, D_MODEL)):
        ref[...] = proj(c0, width).astype(BF16)
        c0 += width


def _inproj(x2, mod3, pos2, g, freq, sgn, w_bf, seq):
    t, d = x2.shape
    tm = TM_IN
    tiles_per_seq = seq // tm
    widths = (RET_QK, RET_QK, RET_V, RET_V, SB_W, SB_W, SB_W, D_MODEL, D_MODEL)
    row = lambda i: (i, 0)
    return pl.pallas_call(
        _inproj_kernel,
        out_shape=[jax.ShapeDtypeStruct((t, w), BF16) for w in widths],
        grid=(t // tm,),
        in_specs=[pl.BlockSpec((tm, d), row),
                  pl.BlockSpec((1, 6, d), lambda i: (i // tiles_per_seq, 0, 0)),
                  pl.BlockSpec((tm, 1), row),
                  _resident((1, d)),
                  _resident((1, LANES)),
                  _resident((1, LANES)),
                  _resident((d, D_IN))],
        out_specs=[pl.BlockSpec((tm, w), row) for w in widths],
        compiler_params=pltpu.CompilerParams(dimension_semantics=("arbitrary",),
                                             vmem_limit_bytes=VMEM_LIMIT),
        name="inproj",
    )(x2, mod3, pos2, g, freq, sgn, w_bf)


def _ret_kernel(lg_ref, q_ref, k_ref, v_ref, g_ref, gn_ref, o_ref, state_ref, dtab_ref):
    b = pl.program_id(0)
    j = pl.program_id(1)
    blk = RET_L

    @pl.when(jnp.logical_and(b == 0, j == 0))
    def _():
        ri = lax.broadcasted_iota(jnp.int32, (blk, blk), 0)
        ci = lax.broadcasted_iota(jnp.int32, (blk, blk), 1)
        dist = jnp.abs(ri - ci).astype(F32)
        keep = (ci // CHUNK) <= (ri // CHUNK)
        for h in range(RET_HEADS):
            dtab_ref[h] = jnp.where(keep, jnp.exp(dist * lg_ref[h]), 0.0)

    @pl.when(j == 0)
    def _():
        state_ref[...] = jnp.zeros_like(state_ref)

    lane = lax.broadcasted_iota(jnp.int32, (1, LANES), 1)
    idx = lax.broadcasted_iota(jnp.int32, (blk, LANES), 0).astype(F32)
    for h in range(RET_HEADS):
        p = h // 2
        lg = lg_ref[h]
        in_head = (lane // RET_DQK) == (h % 2)
        qp = q_ref[:, p * LANES:(p + 1) * LANES]
        kp = k_ref[:, p * LANES:(p + 1) * LANES]
        vh = v_ref[:, h * RET_DV:(h + 1) * RET_DV]
        qh = jnp.where(in_head, qp, jnp.zeros_like(qp))
        st = state_ref[h]

        scores = _dot_nt(qh, kp) * dtab_ref[h]
        intra = _dot(scores.astype(BF16), vh)
        cross = _dot(qh, st.astype(BF16)) * jnp.exp((idx + 1.0) * lg)
        kd = (kp.astype(F32) * jnp.exp((blk - 1.0 - idx) * lg)).astype(BF16)
        state_ref[h] = st * jnp.exp(blk * lg) + _dot_tn(kd, vh)

        o = intra + cross
        mu = jnp.mean(o, axis=-1, keepdims=True)
        dlt = o - mu
        var = jnp.mean(dlt * dlt, axis=-1, keepdims=True)
        y = dlt * lax.rsqrt(var + EPS) * gn_ref[:, h * RET_DV:(h + 1) * RET_DV]
        g = g_ref[:, h * RET_DV:(h + 1) * RET_DV].astype(F32)
        o_ref[:, h * RET_DV:(h + 1) * RET_DV] = (g * jax.nn.sigmoid(g) * y).astype(BF16)


def _retention(log_gamma, q, k, v, g, gn, bsz, seq):
    t = q.shape[0]
    nb = seq // RET_L
    row = lambda b, j: (b * nb + j, 0)
    return pl.pallas_call(
        _ret_kernel,
        out_shape=jax.ShapeDtypeStruct((t, RET_V), BF16),
        grid=(bsz, nb),
        in_specs=[pl.BlockSpec(memory_space=pltpu.SMEM),
                  pl.BlockSpec((RET_L, RET_QK), row),
                  pl.BlockSpec((RET_L, RET_QK), row),
                  pl.BlockSpec((RET_L, RET_V), row),
                  pl.BlockSpec((RET_L, RET_V), row),
                  pl.BlockSpec((1, RET_V), lambda b, j: (0, 0))],
        out_specs=pl.BlockSpec((RET_L, RET_V), row),
        scratch_shapes=[pltpu.VMEM((RET_HEADS, LANES, RET_DV), F32),
                        pltpu.VMEM((RET_HEADS, RET_L, RET_L), F32)],
        compiler_params=pltpu.CompilerParams(dimension_semantics=("arbitrary", "arbitrary"),
                                             vmem_limit_bytes=VMEM_LIMIT),
        name="retention",
    )(log_gamma, q, k, v, g, gn)


def _sb_kernel(q_ref, k_ref, v_ref, o_ref, qm_ref, tri_ref, acc_ref, r_ref):
    qi = pl.program_id(1)
    t = SB_T
    lane = lax.broadcasted_iota(jnp.int32, (1, LANES), 1)
    ri = lax.broadcasted_iota(jnp.int32, (t, t), 0)
    ci = lax.broadcasted_iota(jnp.int32, (t, t), 1)

    @pl.when(jnp.logical_and(pl.program_id(0) == 0, qi == 0))
    def _():
        tri_ref[...] = jnp.where(ri > ci, 1.0, 0.0).astype(BF16)

    for h in range(SB_HEADS):
        qp = q_ref[:, (h // 2) * LANES:(h // 2 + 1) * LANES]
        in_head = (lane // SB_DH) == (h % 2)
        qm_ref[h] = jnp.where(in_head, qp, jnp.zeros_like(qp)) * (SB_DH ** -0.5)

    def stage_scores(h, start, diag):
        cols = slice((h // 2) * LANES, (h // 2 + 1) * LANES)
        z = _dot_nt(qm_ref[h], k_ref[pl.ds(start, t), cols])
        sp = jnp.maximum(z, 0.0) + jnp.log(1.0 + jnp.exp2(jnp.abs(z) * (-LOG2E)))
        d = z - sp
        if diag:
            sp = jnp.where(ci < ri, sp, 0.0)
        rs = jnp.broadcast_to(jnp.sum(sp, axis=-1, keepdims=True), (t, LANES))
        if not diag:
            d = d - jnp.concatenate([r_ref[h]] * (t // LANES), axis=-1)
            r_ref[h] += rs
        else:
            r_ref[h] = rs
        return sp.astype(BF16), d

    def stage_weights(spb, d, diag):
        a = jnp.exp(d - _dot(spb, tri_ref[...]))
        if diag:
            a = jnp.where(ci < ri, a, 0.0)
        return a.astype(BF16)

    def stage_values(h, ab, start, diag):
        cols = slice((h // 2) * LANES, (h // 2 + 1) * LANES)
        pv = _dot(ab, v_ref[pl.ds(start, t), cols])
        if diag:
            acc_ref[h] = pv
        else:
            acc_ref[h] += pv

    def sweep(start, diag):
        s1 = {}
        s2 = {}
        for step in range(SB_HEADS + 2 * SB_SKEW):
            if step < SB_HEADS:
                s1[step] = stage_scores(step, start, diag)
            if 0 <= step - SB_SKEW < SB_HEADS:
                s2[step - SB_SKEW] = stage_weights(*s1.pop(step - SB_SKEW), diag)
            if 0 <= step - 2 * SB_SKEW < SB_HEADS:
                stage_values(step - 2 * SB_SKEW, s2.pop(step - 2 * SB_SKEW), start, diag)

    sweep(pl.multiple_of(qi * t, t), True)

    def cond(carry):
        it, live = carry
        return jnp.logical_and(it < qi, live)

    def body(carry):
        it, _ = carry
        start = pl.multiple_of((qi - 1 - it) * t, t)
        sweep(start, False)
        r_min = r_ref[0]
        for h in range(1, SB_HEADS):
            r_min = jnp.minimum(r_min, r_ref[h])
        return it + 1, jnp.min(r_min) < SB_SKIP_LOG

    lax.while_loop(cond, body, (jnp.int32(0), True))

    for p in range(SB_HEADS // 2):
        o_ref[:, p * LANES:(p + 1) * LANES] = jnp.where(
            lane < SB_DH, acc_ref[2 * p], acc_ref[2 * p + 1]).astype(BF16)


def _stick_breaking(q, k, v, bsz, seq):
    t = q.shape[0]
    nq = seq // SB_T
    return pl.pallas_call(
        _sb_kernel,
        out_shape=jax.ShapeDtypeStruct((t, SB_W), BF16),
        grid=(bsz, nq),
        in_specs=[pl.BlockSpec((SB_T, SB_W), lambda b, i: (b * nq + i, 0)),
                  pl.BlockSpec((seq, SB_W), lambda b, i: (b, 0)),
                  pl.BlockSpec((seq, SB_W), lambda b, i: (b, 0))],
        out_specs=pl.BlockSpec((SB_T, SB_W), lambda b, i: (b * nq + i, 0)),
        scratch_shapes=[pltpu.VMEM((SB_HEADS, SB_T, LANES), BF16),
                        pltpu.VMEM((SB_T, SB_T), BF16),
                        pltpu.VMEM((SB_HEADS, SB_T, LANES), F32),
                        pltpu.VMEM((SB_HEADS, SB_T, LANES), F32)],
        compiler_params=pltpu.CompilerParams(dimension_semantics=("arbitrary", "arbitrary"),
                                             vmem_limit_bytes=VMEM_LIMIT),
        name="stickbreak",
    )(q, k, v)


def _post_kernel(x_ref, mod_ref, ret_ref, sb_ref, ar_ref, as_ref,
                 g_mix_ref, g_pre_ref, g_post_ref,
                 w_ret_ref, w_sb_ref, w_out_ref, w_ff1_ref, w_ff2_ref, o_ref):
    gt1 = mod_ref[0, 2:3, :]
    sh2 = mod_ref[0, 3:4, :]
    sc2 = mod_ref[0, 4:5, :]
    gt2 = mod_ref[0, 5:6, :]

    mixed = (jax.nn.sigmoid(ar_ref[...].astype(F32)) * _dot(ret_ref[...], w_ret_ref[...])
             + jax.nn.sigmoid(as_ref[...].astype(F32)) * _dot(sb_ref[...], w_sb_ref[...]))
    y = _dot(mixed.astype(BF16), w_out_ref[...])
    h_res = x_ref[...] + gt1 * _rms(y, g_mix_ref[...])

    h2 = (_rms(h_res, g_pre_ref[...]) * (1.0 + sc2) + sh2).astype(BF16)
    f = None
    fc = 1024
    for c in range(D_FF // fc):
        u = jnp.maximum(_dot(h2, w_ff1_ref[:, c * fc:(c + 1) * fc]), 0.0)
        part = _dot((u * u).astype(BF16), w_ff2_ref[c * fc:(c + 1) * fc, :])
        f = part if f is None else f + part
    o_ref[...] = h_res + gt2 * _rms(f, g_post_ref[...])


def _post(x2, mod3, ret, sb, a_r, a_s, g_mix, g_pre, g_post, w_ret, w_sb, w_out, w_ff1, w_ff2, seq):
    t, d = x2.shape
    tm = TM_POST
    tiles_per_seq = seq // tm
    row = lambda i: (i, 0)
    return pl.pallas_call(
        _post_kernel,
        out_shape=jax.ShapeDtypeStruct((t, d), F32),
        grid=(t // tm,),
        in_specs=[pl.BlockSpec((tm, d), row),
                  pl.BlockSpec((1, 6, d), lambda i: (i // tiles_per_seq, 0, 0)),
                  pl.BlockSpec((tm, RET_V), row),
                  pl.BlockSpec((tm, SB_W), row),
                  pl.BlockSpec((tm, d), row),
                  pl.BlockSpec((tm, d), row),
                  _resident((1, d)), _resident((1, d)), _resident((1, d)),
                  _resident((RET_V, d)), _resident((SB_W, d)), _resident((d, d)),
                  _resident((d, D_FF)), _resident((D_FF, d))],
        out_specs=pl.BlockSpec((tm, d), row),
        compiler_params=pltpu.CompilerParams(dimension_semantics=("arbitrary",),
                                             vmem_limit_bytes=VMEM_LIMIT),
        name="post",
    )(x2, mod3, ret, sb, a_r, a_s, g_mix, g_pre, g_post, w_ret, w_sb, w_out, w_ff1, w_ff2)


def kernel(x, c, positions, ada_w, ada_b, pre_mix_g, post_mix_g, pre_ffn_g, post_ffn_g,
           w_in, ret_gn_g, w_ret_branch, w_sb_branch, w_out, w_ff1, w_ff2):
    bsz, seq, d = x.shape
    depth = ada_w.shape[0]
    assert d == D_MODEL and seq % TM_IN == 0 and seq % TM_POST == 0
    assert seq % RET_L == 0 and seq % SB_T == 0 and RET_L % CHUNK == 0
    t = bsz * seq

    inv_freq = ROPE_BASE ** (-jnp.arange(0, RET_DQK, 2, dtype=F32) / RET_DQK)
    freq = jnp.tile(inv_freq, LANES // (RET_DQK // 2)).reshape(1, LANES)
    sgn = jnp.tile(jnp.concatenate([-jnp.ones(RET_DQK // 2, F32), jnp.ones(RET_DQK // 2, F32)]),
                   LANES // RET_DQK).reshape(1, LANES)
    log_gamma = jnp.log1p(-(2.0 ** (-5.0 - jnp.arange(RET_HEADS, dtype=F32))))
    pos2 = positions.reshape(t, 1)

    h_res = x.astype(F32).reshape(t, d)
    for l in range(depth):
        mod3 = _ada(c.astype(F32), ada_w[l], ada_b[l]).reshape(bsz, 6, d)
        q_r, k_r, v_r, g_r, q_s, k_s, v_s, a_r, a_s = _inproj(
            h_res, mod3, pos2, pre_mix_g[l].reshape(1, d), freq, sgn, w_in[l].astype(BF16), seq)
        ret = _retention(log_gamma, q_r, k_r, v_r, g_r, ret_gn_g[l].reshape(1, RET_V), bsz, seq)
        sb = _stick_breaking(q_s, k_s, v_s, bsz, seq)
        h_res = _post(h_res, mod3, ret, sb, a_r, a_s,
                      post_mix_g[l].reshape(1, d), pre_ffn_g[l].reshape(1, d),
                      post_ffn_g[l].reshape(1, d),
                      w_ret_branch[l].astype(BF16), w_sb_branch[l].astype(BF16),
                      w_out[l].astype(BF16), w_ff1[l].astype(BF16), w_ff2[l].astype(BF16), seq)
    return h_res.reshape(bsz, seq, d).astype(x.dtype)
```

```python
import jax
import jax.numpy as jnp
from jax import lax
from jax.experimental import pallas as pl
from jax.experimental.pallas import tpu as pltpu

D_MODEL = 1024
CHUNK = 64
RET_HEADS = 8
RET_DQK = 64
RET_DV = 128
SB_HEADS = 8
SB_DH = 64
D_FF = 4 * D_MODEL
ROPE_BASE = 10000.0
EPS = 1e-6

RET_QK = RET_HEADS * RET_DQK
RET_V = RET_HEADS * RET_DV
SB_W = SB_HEADS * SB_DH
D_IN = 2 * RET_QK + 2 * RET_V + 3 * SB_W + 2 * D_MODEL

LANES = 128
VMEM_LIMIT = 56 * 1024 * 1024

TM_IN = 512
TM_POST = 512
RET_L = 256
SB_T = 256
SB_SKIP_LOG = 105.0
SB_SKEW = 1
LOG2E = 1.4426950408889634

F32 = jnp.float32
BF16 = jnp.bfloat16


def _dot(a, b):
    return jnp.dot(a, b, preferred_element_type=F32)


def _dot_nt(a, b):
    return lax.dot_general(a, b, (((1,), (1,)), ((), ())), preferred_element_type=F32)


def _dot_tn(a, b):
    return lax.dot_general(a, b, (((0,), (0,)), ((), ())), preferred_element_type=F32)


def _rms(x, g):
    return x * lax.rsqrt(jnp.mean(x * x, axis=-1, keepdims=True) + EPS) * g


def _resident(shape):
    nd = len(shape)
    return pl.BlockSpec(shape, lambda *_: (0,) * nd, pipeline_mode=pl.Buffered(1))


def _ada_kernel(c_ref, w_ref, b_ref, o_ref):
    c = c_ref[...]
    s = (c * jax.nn.sigmoid(c)).astype(BF16)
    o_ref[...] = _dot(s, w_ref[...].astype(BF16)) + b_ref[...]


def _ada(c, w, b):
    bsz, d = c.shape
    n = w.shape[1]
    tn = 1024
    return pl.pallas_call(
        _ada_kernel,
        out_shape=jax.ShapeDtypeStruct((bsz, n), F32),
        grid=(n // tn,),
        in_specs=[pl.BlockSpec((bsz, d), lambda j: (0, 0)),
                  pl.BlockSpec((d, tn), lambda j: (0, j)),
                  pl.BlockSpec((1, tn), lambda j: (0, j))],
        out_specs=pl.BlockSpec((bsz, tn), lambda j: (0, j)),
        compiler_params=pltpu.CompilerParams(dimension_semantics=("arbitrary",)),
        name="ada",
    )(c, w, b.reshape(1, n))


def _rope_kernel(pos_ref, freq_ref, cos_ref, sin_ref):
    ang = freq_ref[...] * pos_ref[0].astype(F32)
    cos = jnp.cos(ang)
    sin = jnp.sin(ang)
    pairs = LANES // RET_DQK
    cos_ref[...] = jnp.concatenate([cos, cos] * pairs, axis=0).T
    sin_ref[...] = jnp.concatenate([-sin, sin] * pairs, axis=0).T


def _rope_tables(positions, inv_freq):
    bsz, seq = positions.shape
    nf = inv_freq.shape[0]
    return pl.pallas_call(
        _rope_kernel,
        out_shape=[jax.ShapeDtypeStruct((bsz * seq, LANES), F32)] * 2,
        grid=(bsz,),
        in_specs=[pl.BlockSpec((1, 1, seq), lambda b: (b, 0, 0)),
                  pl.BlockSpec((nf, 1), lambda b: (0, 0))],
        out_specs=[pl.BlockSpec((seq, LANES), lambda b: (b, 0))] * 2,
        compiler_params=pltpu.CompilerParams(dimension_semantics=("arbitrary",)),
        name="rope",
    )(positions.reshape(bsz, 1, seq), inv_freq.reshape(nf, 1))


def _inproj_kernel(x_ref, mod_ref, cos_ref, sin_ref, g_ref, w_ref,
                   qr_ref, kr_ref, vr_ref, gr_ref, qs_ref, ks_ref, vs_ref, ar_ref, as_ref):
    h = _rms(x_ref[...], g_ref[...])
    h = h * (1.0 + mod_ref[0, 1:2, :]) + mod_ref[0, 0:1, :]
    hb = h.astype(BF16)

    cos = cos_ref[...]
    sin = sin_ref[...]
    lane = lax.broadcasted_iota(jnp.int32, (1, LANES), 1)
    first_half = (lane % RET_DQK) < (RET_DQK // 2)

    def rope(p):
        rot = jnp.where(first_half, pltpu.roll(p, LANES - RET_DQK // 2, 1),
                        pltpu.roll(p, RET_DQK // 2, 1))
        return p * cos + rot * sin

    def proj(c0, width):
        return _dot(hb, w_ref[:, c0:c0 + width])

    c0 = 0
    acc = proj(c0, RET_QK)
    for j in range(RET_QK // LANES):
        qr_ref[:, j * LANES:(j + 1) * LANES] = rope(acc[:, j * LANES:(j + 1) * LANES]).astype(BF16)
    c0 += RET_QK
    acc = proj(c0, RET_QK)
    kscale = RET_DQK ** -0.5
    for j in range(RET_QK // LANES):
        kr_ref[:, j * LANES:(j + 1) * LANES] = (
            rope(acc[:, j * LANES:(j + 1) * LANES]) * kscale).astype(BF16)
    c0 += RET_QK
    for ref, width in ((vr_ref, RET_V), (gr_ref, RET_V), (qs_ref, SB_W), (ks_ref, SB_W),
                       (vs_ref, SB_W), (ar_ref, D_MODEL), (as_ref, D_MODEL)):
        ref[...] = proj(c0, width).astype(BF16)
        c0 += width


def _inproj(x2, mod3, cos, sin, g, w_bf, seq):
    t, d = x2.shape
    tm = TM_IN
    tiles_per_seq = seq // tm
    widths = (RET_QK, RET_QK, RET_V, RET_V, SB_W, SB_W, SB_W, D_MODEL, D_MODEL)
    row = lambda i: (i, 0)
    return pl.pallas_call(
        _inproj_kernel,
        out_shape=[jax.ShapeDtypeStruct((t, w), BF16) for w in widths],
        grid=(t // tm,),
        in_specs=[pl.BlockSpec((tm, d), row),
                  pl.BlockSpec((1, 6, d), lambda i: (i // tiles_per_seq, 0, 0)),
                  pl.BlockSpec((tm, LANES), row),
                  pl.BlockSpec((tm, LANES), row),
                  _resident((1, d)),
                  _resident((d, D_IN))],
        out_specs=[pl.BlockSpec((tm, w), row) for w in widths],
        compiler_params=pltpu.CompilerParams(dimension_semantics=("arbitrary",),
                                             vmem_limit_bytes=VMEM_LIMIT),
        name="inproj",
    )(x2, mod3, cos, sin, g, w_bf)


def _ret_kernel(lg_ref, q_ref, k_ref, v_ref, g_ref, gn_ref, o_ref,
                state_ref, dtab_ref, qdec_ref, kdec_ref):
    b = pl.program_id(0)
    j = pl.program_id(1)
    blk = RET_L

    @pl.when(jnp.logical_and(b == 0, j == 0))
    def _():
        ri = lax.broadcasted_iota(jnp.int32, (blk, blk), 0)
        ci = lax.broadcasted_iota(jnp.int32, (blk, blk), 1)
        dist = jnp.abs(ri - ci).astype(F32)
        keep = (ci // CHUNK) <= (ri // CHUNK)
        idx = lax.broadcasted_iota(jnp.int32, (blk, LANES), 0).astype(F32)
        for h in range(RET_HEADS):
            dtab_ref[h] = jnp.where(keep, jnp.exp(dist * lg_ref[h]), 0.0)
            qdec_ref[h] = jnp.exp((idx + 1.0) * lg_ref[h])
            kdec_ref[h] = jnp.exp((blk - 1.0 - idx) * lg_ref[h])

    @pl.when(j == 0)
    def _():
        state_ref[...] = jnp.zeros_like(state_ref)

    lane = lax.broadcasted_iota(jnp.int32, (1, LANES), 1)
    for h in range(RET_HEADS):
        p = h // 2
        lg = lg_ref[h]
        in_head = (lane // RET_DQK) == (h % 2)
        qp = q_ref[:, p * LANES:(p + 1) * LANES]
        kp = k_ref[:, p * LANES:(p + 1) * LANES]
        vh = v_ref[:, h * RET_DV:(h + 1) * RET_DV]
        qh = jnp.where(in_head, qp, jnp.zeros_like(qp))
        st = state_ref[h]

        scores = _dot_nt(qh, kp) * dtab_ref[h]
        intra = _dot(scores.astype(BF16), vh)
        cross = _dot(qh, st.astype(BF16)) * qdec_ref[h]
        kd = (kp.astype(F32) * kdec_ref[h]).astype(BF16)
        state_ref[h] = st * jnp.exp(blk * lg) + _dot_tn(kd, vh)

        o = intra + cross
        mu = jnp.mean(o, axis=-1, keepdims=True)
        dlt = o - mu
        var = jnp.mean(dlt * dlt, axis=-1, keepdims=True)
        y = dlt * lax.rsqrt(var + EPS) * gn_ref[:, h * RET_DV:(h + 1) * RET_DV]
        g = g_ref[:, h * RET_DV:(h + 1) * RET_DV].astype(F32)
        o_ref[:, h * RET_DV:(h + 1) * RET_DV] = (g * jax.nn.sigmoid(g) * y).astype(BF16)


def _retention(log_gamma, q, k, v, g, gn, bsz, seq):
    t = q.shape[0]
    nb = seq // RET_L
    row = lambda b, j: (b * nb + j, 0)
    return pl.pallas_call(
        _ret_kernel,
        out_shape=jax.ShapeDtypeStruct((t, RET_V), BF16),
        grid=(bsz, nb),
        in_specs=[pl.BlockSpec(memory_space=pltpu.SMEM),
                  pl.BlockSpec((RET_L, RET_QK), row),
                  pl.BlockSpec((RET_L, RET_QK), row),
                  pl.BlockSpec((RET_L, RET_V), row),
                  pl.BlockSpec((RET_L, RET_V), row),
                  pl.BlockSpec((1, RET_V), lambda b, j: (0, 0))],
        out_specs=pl.BlockSpec((RET_L, RET_V), row),
        scratch_shapes=[pltpu.VMEM((RET_HEADS, LANES, RET_DV), F32),
                        pltpu.VMEM((RET_HEADS, RET_L, RET_L), F32),
                        pltpu.VMEM((RET_HEADS, RET_L, LANES), F32),
                        pltpu.VMEM((RET_HEADS, RET_L, LANES), F32)],
        compiler_params=pltpu.CompilerParams(dimension_semantics=("arbitrary", "arbitrary"),
                                             vmem_limit_bytes=VMEM_LIMIT),
        name="retention",
    )(log_gamma, q, k, v, g, gn)


def _sb_kernel(q_ref, k_ref, v_ref, o_ref, qm_ref, tri_ref, acc_ref, r_ref):
    qi = pl.program_id(1)
    t = SB_T
    lane = lax.broadcasted_iota(jnp.int32, (1, LANES), 1)

    @pl.when(jnp.logical_and(pl.program_id(0) == 0, qi == 0))
    def _():
        ri = lax.broadcasted_iota(jnp.int32, (t, t), 0)
        ci = lax.broadcasted_iota(jnp.int32, (t, t), 1)
        tri_ref[...] = jnp.where(ri > ci, 1.0, 0.0).astype(BF16)

    for h in range(SB_HEADS):
        qp = q_ref[:, (h // 2) * LANES:(h // 2 + 1) * LANES]
        in_head = (lane // SB_DH) == (h % 2)
        qm_ref[h] = jnp.where(in_head, qp, jnp.zeros_like(qp)) * (SB_DH ** -0.5)

    half = t // 2
    rt = lax.broadcasted_iota(jnp.int32, (half, half), 0)
    ct = lax.broadcasted_iota(jnp.int32, (half, half), 1)
    rb = lax.broadcasted_iota(jnp.int32, (half, t), 0) + half
    cb = lax.broadcasted_iota(jnp.int32, (half, t), 1)
    valid_top = ct < rt
    valid_bot = cb < rb

    def softplus_parts(z):
        sp = jnp.maximum(z, 0.0) + jnp.log(1.0 + jnp.exp2(jnp.abs(z) * (-LOG2E)))
        return sp, z - sp

    def stage_scores(h, start, diag):
        cols = slice((h // 2) * LANES, (h // 2 + 1) * LANES)
        z = _dot_nt(qm_ref[h], k_ref[pl.ds(start, t), cols])
        if diag:
            sp_t, d_t = softplus_parts(z[:half, :half])
            sp_b, d_b = softplus_parts(z[half:, :])
            sp_t = jnp.where(valid_top, sp_t, 0.0)
            sp_b = jnp.where(valid_bot, sp_b, 0.0)
            sp = jnp.concatenate(
                [jnp.concatenate([sp_t, jnp.zeros_like(sp_t)], axis=1), sp_b], axis=0)
            d = (d_t, d_b)
        else:
            sp, d = softplus_parts(z)
            d = d - jnp.concatenate([r_ref[h]] * (t // LANES), axis=-1)
        rs = jnp.broadcast_to(jnp.sum(sp, axis=-1, keepdims=True), (t, LANES))
        if diag:
            r_ref[h] = rs
        else:
            r_ref[h] += rs
        return sp.astype(BF16), d

    def stage_weights(spb, d, diag):
        later = _dot(spb, tri_ref[...])
        if diag:
            d_t, d_b = d
            a_t = jnp.where(valid_top, jnp.exp(d_t - later[:half, :half]), 0.0)
            a_b = jnp.where(valid_bot, jnp.exp(d_b - later[half:, :]), 0.0)
            a = jnp.concatenate(
                [jnp.concatenate([a_t, jnp.zeros_like(a_t)], axis=1), a_b], axis=0)
        else:
            a = jnp.exp(d - later)
        return a.astype(BF16)

    def stage_values(h, ab, start, diag):
        cols = slice((h // 2) * LANES, (h // 2 + 1) * LANES)
        pv = _dot(ab, v_ref[pl.ds(start, t), cols])
        if diag:
            acc_ref[h] = pv
        else:
            acc_ref[h] += pv

    def sweep(start, diag):
        s1 = {}
        s2 = {}
        for step in range(SB_HEADS + 2 * SB_SKEW):
            if step < SB_HEADS:
                s1[step] = stage_scores(step, start, diag)
            if 0 <= step - SB_SKEW < SB_HEADS:
                s2[step - SB_SKEW] = stage_weights(*s1.pop(step - SB_SKEW), diag)
            if 0 <= step - 2 * SB_SKEW < SB_HEADS:
                stage_values(step - 2 * SB_SKEW, s2.pop(step - 2 * SB_SKEW), start, diag)

    sweep(pl.multiple_of(qi * t, t), True)

    def cond(carry):
        it, live = carry
        return jnp.logical_and(it < qi, live)

    def body(carry):
        it, _ = carry
        start = pl.multiple_of((qi - 1 - it) * t, t)
        sweep(start, False)
        r_min = r_ref[0]
        for h in range(1, SB_HEADS):
            r_min = jnp.minimum(r_min, r_ref[h])
        return it + 1, jnp.min(r_min) < SB_SKIP_LOG

    lax.while_loop(cond, body, (jnp.int32(0), True))

    for p in range(SB_HEADS // 2):
        o_ref[:, p * LANES:(p + 1) * LANES] = jnp.where(
            lane < SB_DH, acc_ref[2 * p], acc_ref[2 * p + 1]).astype(BF16)


def _stick_breaking(q, k, v, bsz, seq):
    t = q.shape[0]
    nq = seq // SB_T
    return pl.pallas_call(
        _sb_kernel,
        out_shape=jax.ShapeDtypeStruct((t, SB_W), BF16),
        grid=(bsz, nq),
        in_specs=[pl.BlockSpec((SB_T, SB_W), lambda b, i: (b * nq + i, 0)),
                  pl.BlockSpec((seq, SB_W), lambda b, i: (b, 0)),
                  pl.BlockSpec((seq, SB_W), lambda b, i: (b, 0))],
        out_specs=pl.BlockSpec((SB_T, SB_W), lambda b, i: (b * nq + i, 0)),
        scratch_shapes=[pltpu.VMEM((SB_HEADS, SB_T, LANES), BF16),
                        pltpu.VMEM((SB_T, SB_T), BF16),
                        pltpu.VMEM((SB_HEADS, SB_T, LANES), F32),
                        pltpu.VMEM((SB_HEADS, SB_T, LANES), F32)],
        compiler_params=pltpu.CompilerParams(dimension_semantics=("arbitrary", "arbitrary"),
                                             vmem_limit_bytes=VMEM_LIMIT),
        name="stickbreak",
    )(q, k, v)


def _post_kernel(x_ref, mod_ref, ret_ref, sb_ref, ar_ref, as_ref,
                 g_mix_ref, g_pre_ref, g_post_ref,
                 w_ret_ref, w_sb_ref, w_out_ref, w_ff1_ref, w_ff2_ref, o_ref):
    gt1 = mod_ref[0, 2:3, :]
    sh2 = mod_ref[0, 3:4, :]
    sc2 = mod_ref[0, 4:5, :]
    gt2 = mod_ref[0, 5:6, :]

    mixed = (jax.nn.sigmoid(ar_ref[...].astype(F32)) * _dot(ret_ref[...], w_ret_ref[...])
             + jax.nn.sigmoid(as_ref[...].astype(F32)) * _dot(sb_ref[...], w_sb_ref[...]))
    y = _dot(mixed.astype(BF16), w_out_ref[...])
    h_res = x_ref[...] + gt1 * _rms(y, g_mix_ref[...])

    h2 = (_rms(h_res, g_pre_ref[...]) * (1.0 + sc2) + sh2).astype(BF16)
    f = None
    fc = 1024
    for c in range(D_FF // fc):
        u = jnp.maximum(_dot(h2, w_ff1_ref[:, c * fc:(c + 1) * fc]), 0.0)
        part = _dot((u * u).astype(BF16), w_ff2_ref[c * fc:(c + 1) * fc, :])
        f = part if f is None else f + part
    o_ref[...] = h_res + gt2 * _rms(f, g_post_ref[...])


def _post(x2, mod3, ret, sb, a_r, a_s, g_mix, g_pre, g_post, w_ret, w_sb, w_out, w_ff1, w_ff2, seq):
    t, d = x2.shape
    tm = TM_POST
    tiles_per_seq = seq // tm
    row = lambda i: (i, 0)
    return pl.pallas_call(
        _post_kernel,
        out_shape=jax.ShapeDtypeStruct((t, d), F32),
        grid=(t // tm,),
        in_specs=[pl.BlockSpec((tm, d), row),
                  pl.BlockSpec((1, 6, d), lambda i: (i // tiles_per_seq, 0, 0)),
                  pl.BlockSpec((tm, RET_V), row),
                  pl.BlockSpec((tm, SB_W), row),
                  pl.BlockSpec((tm, d), row),
                  pl.BlockSpec((tm, d), row),
                  _resident((1, d)), _resident((1, d)), _resident((1, d)),
                  _resident((RET_V, d)), _resident((SB_W, d)), _resident((d, d)),
                  _resident((d, D_FF)), _resident((D_FF, d))],
        out_specs=pl.BlockSpec((tm, d), row),
        compiler_params=pltpu.CompilerParams(dimension_semantics=("arbitrary",),
                                             vmem_limit_bytes=VMEM_LIMIT),
        name="post",
    )(x2, mod3, ret, sb, a_r, a_s, g_mix, g_pre, g_post, w_ret, w_sb, w_out, w_ff1, w_ff2)


def kernel(x, c, positions, ada_w, ada_b, pre_mix_g, post_mix_g, pre_ffn_g, post_ffn_g,
           w_in, ret_gn_g, w_ret_branch, w_sb_branch, w_out, w_ff1, w_ff2):
    bsz, seq, d = x.shape
    depth = ada_w.shape[0]
    assert d == D_MODEL and seq % TM_IN == 0 and seq % TM_POST == 0
    assert seq % RET_L == 0 and seq % SB_T == 0 and RET_L % CHUNK == 0
    t = bsz * seq

    inv_freq = ROPE_BASE ** (-jnp.arange(0, RET_DQK, 2, dtype=F32) / RET_DQK)
    log_gamma = jnp.log1p(-(2.0 ** (-5.0 - jnp.arange(RET_HEADS, dtype=F32))))
    cos, sin = _rope_tables(positions, inv_freq)

    h_res = x.astype(F32).reshape(t, d)
    for l in range(depth):
        mod3 = _ada(c.astype(F32), ada_w[l], ada_b[l]).reshape(bsz, 6, d)
        q_r, k_r, v_r, g_r, q_s, k_s, v_s, a_r, a_s = _inproj(
            h_res, mod3, cos, sin, pre_mix_g[l].reshape(1, d), w_in[l].astype(BF16), seq)
        ret = _retention(log_gamma, q_r, k_r, v_r, g_r, ret_gn_g[l].reshape(1, RET_V), bsz, seq)
        sb = _stick_breaking(q_s, k_s, v_s, bsz, seq)
        h_res = _post(h_res, mod3, ret, sb, a_r, a_s,
                      post_mix_g[l].reshape(1, d), pre_ffn_g[l].reshape(1, d),
                      post_ffn_g[l].reshape(1, d),
                      w_ret_branch[l].astype(BF16), w_sb_branch[l].astype(BF16),
                      w_out[l].astype(BF16), w_ff1[l].astype(BF16), w_ff2[l].astype(BF16), seq)
    return h_res.reshape(bsz, seq, d).astype(x.dtype)
```

```python
import functools

import jax
import jax.numpy as jnp
from jax import lax
from jax.experimental import pallas as pl
from jax.experimental.pallas import tpu as pltpu

D_MODEL = 1024
CHUNK = 64
RET_HEADS = 8
RET_DQK = 64
RET_DV = 128
SB_HEADS = 8
SB_DH = 64
D_FF = 4 * D_MODEL
ROPE_BASE = 10000.0
EPS = 1e-6

RET_QK = RET_HEADS * RET_DQK
RET_V = RET_HEADS * RET_DV
SB_W = SB_HEADS * SB_DH
D_IN = 2 * RET_QK + 2 * RET_V + 3 * SB_W + 2 * D_MODEL

LANES = 128
VMEM_LIMIT = 56 * 1024 * 1024

TM_IN = 512
TM_POST = 512
POST_SPLIT = 2
RET_L = 256
SB_T = 256
SB_SKIP_LOG = 105.0
SB_SKEW = 1
LOG2E = 1.4426950408889634

F32 = jnp.float32
BF16 = jnp.bfloat16


def _dot(a, b):
    return jnp.dot(a, b, preferred_element_type=F32)


def _dot_nt(a, b):
    return lax.dot_general(a, b, (((1,), (1,)), ((), ())), preferred_element_type=F32)


def _dot_tn(a, b):
    return lax.dot_general(a, b, (((0,), (0,)), ((), ())), preferred_element_type=F32)


def _rms(x, g):
    return x * lax.rsqrt(jnp.mean(x * x, axis=-1, keepdims=True) + EPS) * g


def _resident(shape):
    nd = len(shape)
    return pl.BlockSpec(shape, lambda *_: (0,) * nd, pipeline_mode=pl.Buffered(1))


def _ada_kernel(c_ref, w_ref, b_ref, o_ref):
    c = c_ref[...]
    s = (c * jax.nn.sigmoid(c)).astype(BF16)
    o_ref[...] = _dot(s, w_ref[...].astype(BF16)) + b_ref[...]


def _ada(c, w, b):
    bsz, d = c.shape
    n = w.shape[1]
    tn = 1024
    return pl.pallas_call(
        _ada_kernel,
        out_shape=jax.ShapeDtypeStruct((bsz, n), F32),
        grid=(n // tn,),
        in_specs=[pl.BlockSpec((bsz, d), lambda j: (0, 0)),
                  pl.BlockSpec((d, tn), lambda j: (0, j)),
                  pl.BlockSpec((1, tn), lambda j: (0, j))],
        out_specs=pl.BlockSpec((bsz, tn), lambda j: (0, j)),
        compiler_params=pltpu.CompilerParams(dimension_semantics=("arbitrary",)),
        name="ada",
    )(c, w, b.reshape(1, n))


def _rope_kernel(pos_ref, freq_ref, cos_ref, sin_ref):
    ang = freq_ref[...] * pos_ref[0].astype(F32)
    cos = jnp.cos(ang)
    sin = jnp.sin(ang)
    pairs = LANES // RET_DQK
    cos_ref[...] = jnp.concatenate([cos, cos] * pairs, axis=0).T
    sin_ref[...] = jnp.concatenate([-sin, sin] * pairs, axis=0).T


def _rope_tables(positions, inv_freq):
    bsz, seq = positions.shape
    nf = inv_freq.shape[0]
    return pl.pallas_call(
        _rope_kernel,
        out_shape=[jax.ShapeDtypeStruct((bsz * seq, LANES), F32)] * 2,
        grid=(bsz,),
        in_specs=[pl.BlockSpec((1, 1, seq), lambda b: (b, 0, 0)),
                  pl.BlockSpec((nf, 1), lambda b: (0, 0))],
        out_specs=[pl.BlockSpec((seq, LANES), lambda b: (b, 0))] * 2,
        compiler_params=pltpu.CompilerParams(dimension_semantics=("arbitrary",)),
        name="rope",
    )(positions.reshape(bsz, 1, seq), inv_freq.reshape(nf, 1))


def _inproj_kernel(lg_ref, x_ref, mod_ref, cos_ref, sin_ref, g_ref, gn_ref, w_ref,
                   ret_ref, qs_ref, ks_ref, vs_ref, ar_ref, as_ref,
                   qr_ref, kr_ref, vr_ref, gr_ref, state_ref, dtab_ref, qdec_ref, kdec_ref,
                   *, tiles_per_seq):
    i = pl.program_id(0)
    blk = RET_L

    @pl.when(i == 0)
    def _():
        ri = lax.broadcasted_iota(jnp.int32, (blk, blk), 0)
        ci = lax.broadcasted_iota(jnp.int32, (blk, blk), 1)
        dist = jnp.abs(ri - ci).astype(F32)
        keep = (ci // CHUNK) <= (ri // CHUNK)
        idx = lax.broadcasted_iota(jnp.int32, (blk, LANES), 0).astype(F32)
        for hd in range(RET_HEADS):
            dtab_ref[hd] = jnp.where(keep, jnp.exp(dist * lg_ref[hd]), 0.0)
            qdec_ref[hd] = jnp.exp((idx + 1.0) * lg_ref[hd])
            kdec_ref[hd] = jnp.exp((blk - 1.0 - idx) * lg_ref[hd])

    @pl.when(lax.rem(i, tiles_per_seq) == 0)
    def _():
        state_ref[...] = jnp.zeros_like(state_ref)

    h = _rms(x_ref[...], g_ref[...])
    h = h * (1.0 + mod_ref[0, 1:2, :]) + mod_ref[0, 0:1, :]
    hb = h.astype(BF16)

    cos = cos_ref[...]
    sin = sin_ref[...]
    lane = lax.broadcasted_iota(jnp.int32, (1, LANES), 1)
    first_half = (lane % RET_DQK) < (RET_DQK // 2)

    def rope(p):
        rot = jnp.where(first_half, pltpu.roll(p, LANES - RET_DQK // 2, 1),
                        pltpu.roll(p, RET_DQK // 2, 1))
        return p * cos + rot * sin

    def proj(c0, width):
        return _dot(hb, w_ref[:, c0:c0 + width])

    c0 = 0
    acc = proj(c0, RET_QK)
    for j in range(RET_QK // LANES):
        qr_ref[:, j * LANES:(j + 1) * LANES] = rope(acc[:, j * LANES:(j + 1) * LANES]).astype(BF16)
    c0 += RET_QK
    acc = proj(c0, RET_QK)
    kscale = RET_DQK ** -0.5
    for j in range(RET_QK // LANES):
        kr_ref[:, j * LANES:(j + 1) * LANES] = (
            rope(acc[:, j * LANES:(j + 1) * LANES]) * kscale).astype(BF16)
    c0 += RET_QK
    for ref, width in ((vr_ref, RET_V), (gr_ref, RET_V)):
        ref[...] = proj(c0, width).astype(BF16)
        c0 += width

    def retention_block(rows):
        for hd in range(RET_HEADS):
            p = hd // 2
            in_head = (lane // RET_DQK) == (hd % 2)
            qp = qr_ref[rows, p * LANES:(p + 1) * LANES]
            kp = kr_ref[rows, p * LANES:(p + 1) * LANES]
            vh = vr_ref[rows, hd * RET_DV:(hd + 1) * RET_DV]
            qh = jnp.where(in_head, qp, jnp.zeros_like(qp))
            st = state_ref[hd]

            scores = _dot_nt(qh, kp) * dtab_ref[hd]
            intra = _dot(scores.astype(BF16), vh)
            cross = _dot(qh, st.astype(BF16)) * qdec_ref[hd]
            kd = (kp.astype(F32) * kdec_ref[hd]).astype(BF16)
            state_ref[hd] = st * jnp.exp(blk * lg_ref[hd]) + _dot_tn(kd, vh)

            o = intra + cross
            mu = jnp.mean(o, axis=-1, keepdims=True)
            dlt = o - mu
            var = jnp.mean(dlt * dlt, axis=-1, keepdims=True)
            y = dlt * lax.rsqrt(var + EPS) * gn_ref[:, hd * RET_DV:(hd + 1) * RET_DV]
            g = gr_ref[rows, hd * RET_DV:(hd + 1) * RET_DV].astype(F32)
            ret_ref[rows, hd * RET_DV:(hd + 1) * RET_DV] = (g * jax.nn.sigmoid(g) * y).astype(BF16)

    rest = ((qs_ref, SB_W), (ks_ref, SB_W), (vs_ref, SB_W), (ar_ref, D_MODEL), (as_ref, D_MODEL))
    nblk = x_ref.shape[0] // blk
    assert nblk <= len(rest)
    for n, (ref, width) in enumerate(rest):
        if n < nblk:
            retention_block(pl.ds(n * blk, blk))
        ref[...] = proj(c0, width).astype(BF16)
        c0 += width


def _inproj(log_gamma, x2, mod3, cos, sin, g, gn, w_bf, seq):
    t, d = x2.shape
    tm = TM_IN
    tiles_per_seq = seq // tm
    widths = (RET_V, SB_W, SB_W, SB_W, D_MODEL, D_MODEL)
    row = lambda i: (i, 0)
    return pl.pallas_call(
        functools.partial(_inproj_kernel, tiles_per_seq=tiles_per_seq),
        out_shape=[jax.ShapeDtypeStruct((t, w), BF16) for w in widths],
        grid=(t // tm,),
        in_specs=[pl.BlockSpec(memory_space=pltpu.SMEM),
                  pl.BlockSpec((tm, d), row),
                  pl.BlockSpec((1, 6, d), lambda i: (i // tiles_per_seq, 0, 0)),
                  pl.BlockSpec((tm, LANES), row),
                  pl.BlockSpec((tm, LANES), row),
                  _resident((1, d)),
                  _resident((1, RET_V)),
                  _resident((d, D_IN))],
        out_specs=[pl.BlockSpec((tm, w), row) for w in widths],
        scratch_shapes=[pltpu.VMEM((tm, RET_QK), BF16), pltpu.VMEM((tm, RET_QK), BF16),
                        pltpu.VMEM((tm, RET_V), BF16), pltpu.VMEM((tm, RET_V), BF16),
                        pltpu.VMEM((RET_HEADS, LANES, RET_DV), F32),
                        pltpu.VMEM((RET_HEADS, RET_L, RET_L), F32),
                        pltpu.VMEM((RET_HEADS, RET_L, LANES), F32),
                        pltpu.VMEM((RET_HEADS, RET_L, LANES), F32)],
        compiler_params=pltpu.CompilerParams(dimension_semantics=("arbitrary",),
                                             vmem_limit_bytes=VMEM_LIMIT),
        name="inproj",
    )(log_gamma, x2, mod3, cos, sin, g, gn, w_bf)


def _sb_kernel(q_ref, k_ref, v_ref, o_ref, qm_ref, tri_ref, acc_ref, r_ref):
    qi = pl.program_id(1)
    t = SB_T
    lane = lax.broadcasted_iota(jnp.int32, (1, LANES), 1)

    @pl.when(jnp.logical_and(pl.program_id(0) == 0, qi == 0))
    def _():
        ri = lax.broadcasted_iota(jnp.int32, (t, t), 0)
        ci = lax.broadcasted_iota(jnp.int32, (t, t), 1)
        tri_ref[...] = jnp.where(ri > ci, 1.0, 0.0).astype(BF16)

    for h in range(SB_HEADS):
        qp = q_ref[:, (h // 2) * LANES:(h // 2 + 1) * LANES]
        in_head = (lane // SB_DH) == (h % 2)
        qm_ref[h] = jnp.where(in_head, qp, jnp.zeros_like(qp)) * (SB_DH ** -0.5)

    ri = lax.broadcasted_iota(jnp.int32, (t, t), 0)
    ci = lax.broadcasted_iota(jnp.int32, (t, t), 1)
    valid = ci < ri

    def stage_scores(h, start, diag):
        cols = slice((h // 2) * LANES, (h // 2 + 1) * LANES)
        z = _dot_nt(qm_ref[h], k_ref[pl.ds(start, t), cols])
        sp = jnp.maximum(z, 0.0) + jnp.log(1.0 + jnp.exp2(jnp.abs(z) * (-LOG2E)))
        d = z - sp
        if diag:
            sp = jnp.where(valid, sp, 0.0)
        else:
            d = d - jnp.concatenate([r_ref[h]] * (t // LANES), axis=-1)
        rs = jnp.broadcast_to(jnp.sum(sp, axis=-1, keepdims=True), (t, LANES))
        if diag:
            r_ref[h] = rs
        else:
            r_ref[h] += rs
        return sp.astype(BF16), d

    def stage_weights(spb, d, diag):
        a = jnp.exp(d - _dot(spb, tri_ref[...]))
        if diag:
            a = jnp.where(valid, a, 0.0)
        return a.astype(BF16)

    def stage_values(h, ab, start, diag):
        cols = slice((h // 2) * LANES, (h // 2 + 1) * LANES)
        pv = _dot(ab, v_ref[pl.ds(start, t), cols])
        if diag:
            acc_ref[h] = pv
        else:
            acc_ref[h] += pv

    def sweep(start, diag):
        s1 = {}
        s2 = {}
        for step in range(SB_HEADS + 2 * SB_SKEW):
            if step < SB_HEADS:
                s1[step] = stage_scores(step, start, diag)
            if 0 <= step - SB_SKEW < SB_HEADS:
                s2[step - SB_SKEW] = stage_weights(*s1.pop(step - SB_SKEW), diag)
            if 0 <= step - 2 * SB_SKEW < SB_HEADS:
                stage_values(step - 2 * SB_SKEW, s2.pop(step - 2 * SB_SKEW), start, diag)

    sweep(pl.multiple_of(qi * t, t), True)

    def cond(carry):
        it, live = carry
        return jnp.logical_and(it < qi, live)

    def body(carry):
        it, _ = carry
        start = pl.multiple_of((qi - 1 - it) * t, t)
        sweep(start, False)
        r_min = r_ref[0]
        for h in range(1, SB_HEADS):
            r_min = jnp.minimum(r_min, r_ref[h])
        return it + 1, jnp.min(r_min) < SB_SKIP_LOG

    lax.while_loop(cond, body, (jnp.int32(0), True))

    for p in range(SB_HEADS // 2):
        o_ref[:, p * LANES:(p + 1) * LANES] = jnp.where(
            lane < SB_DH, acc_ref[2 * p], acc_ref[2 * p + 1]).astype(BF16)


def _stick_breaking(q, k, v, bsz, seq):
    t = q.shape[0]
    nq = seq // SB_T
    return pl.pallas_call(
        _sb_kernel,
        out_shape=jax.ShapeDtypeStruct((t, SB_W), BF16),
        grid=(bsz, nq),
        in_specs=[pl.BlockSpec((SB_T, SB_W), lambda b, i: (b * nq + i, 0)),
                  pl.BlockSpec((seq, SB_W), lambda b, i: (b, 0)),
                  pl.BlockSpec((seq, SB_W), lambda b, i: (b, 0))],
        out_specs=pl.BlockSpec((SB_T, SB_W), lambda b, i: (b * nq + i, 0)),
        scratch_shapes=[pltpu.VMEM((SB_HEADS, SB_T, LANES), BF16),
                        pltpu.VMEM((SB_T, SB_T), BF16),
                        pltpu.VMEM((SB_HEADS, SB_T, LANES), F32),
                        pltpu.VMEM((SB_HEADS, SB_T, LANES), F32)],
        compiler_params=pltpu.CompilerParams(dimension_semantics=("arbitrary", "arbitrary"),
                                             vmem_limit_bytes=VMEM_LIMIT),
        name="stickbreak",
    )(q, k, v)


def _post_kernel(x_ref, mod_ref, ret_ref, sb_ref, ar_ref, as_ref,
                 g_mix_ref, g_pre_ref, g_post_ref,
                 w_ret_ref, w_sb_ref, w_out_ref, w_ff1_ref, w_ff2_ref, o_ref):
    gt1 = mod_ref[0, 2:3, :]
    sh2 = mod_ref[0, 3:4, :]
    sc2 = mod_ref[0, 4:5, :]
    gt2 = mod_ref[0, 5:6, :]

    def mix(rows):
        mixed = (jax.nn.sigmoid(ar_ref[rows, :].astype(F32)) * _dot(ret_ref[rows, :], w_ret_ref[...])
                 + jax.nn.sigmoid(as_ref[rows, :].astype(F32)) * _dot(sb_ref[rows, :], w_sb_ref[...]))
        y = _dot(mixed.astype(BF16), w_out_ref[...])
        h_res = x_ref[rows, :] + gt1 * _rms(y, g_mix_ref[...])
        h2 = (_rms(h_res, g_pre_ref[...]) * (1.0 + sc2) + sh2).astype(BF16)
        return h_res, h2

    def mlp(h2):
        f = None
        fc = 1024
        for c in range(D_FF // fc):
            u = jnp.maximum(_dot(h2, w_ff1_ref[:, c * fc:(c + 1) * fc]), 0.0)
            part = _dot((u * u).astype(BF16), w_ff2_ref[c * fc:(c + 1) * fc, :])
            f = part if f is None else f + part
        return f

    tm = x_ref.shape[0]
    groups = [pl.ds(i * (tm // POST_SPLIT), tm // POST_SPLIT) for i in range(POST_SPLIT)]
    mixed = [mix(rows) for rows in groups]
    for rows, (h_res, h2) in zip(groups, mixed):
        o_ref[rows, :] = h_res + gt2 * _rms(mlp(h2), g_post_ref[...])


def _post(x2, mod3, ret, sb, a_r, a_s, g_mix, g_pre, g_post, w_ret, w_sb, w_out, w_ff1, w_ff2, seq):
    t, d = x2.shape
    tm = TM_POST
    tiles_per_seq = seq // tm
    row = lambda i: (i, 0)
    return pl.pallas_call(
        _post_kernel,
        out_shape=jax.ShapeDtypeStruct((t, d), F32),
        grid=(t // tm,),
        in_specs=[pl.BlockSpec((tm, d), row),
                  pl.BlockSpec((1, 6, d), lambda i: (i // tiles_per_seq, 0, 0)),
                  pl.BlockSpec((tm, RET_V), row),
                  pl.BlockSpec((tm, SB_W), row),
                  pl.BlockSpec((tm, d), row),
                  pl.BlockSpec((tm, d), row),
                  _resident((1, d)), _resident((1, d)), _resident((1, d)),
                  _resident((RET_V, d)), _resident((SB_W, d)), _resident((d, d)),
                  _resident((d, D_FF)), _resident((D_FF, d))],
        out_specs=pl.BlockSpec((tm, d), row),
        compiler_params=pltpu.CompilerParams(dimension_semantics=("arbitrary",),
                                             vmem_limit_bytes=VMEM_LIMIT),
        name="post",
    )(x2, mod3, ret, sb, a_r, a_s, g_mix, g_pre, g_post, w_ret, w_sb, w_out, w_ff1, w_ff2)


def kernel(x, c, positions, ada_w, ada_b, pre_mix_g, post_mix_g, pre_ffn_g, post_ffn_g,
           w_in, ret_gn_g, w_ret_branch, w_sb_branch, w_out, w_ff1, w_ff2):
    bsz, seq, d = x.shape
    depth = ada_w.shape[0]
    assert d == D_MODEL and seq % TM_IN == 0 and seq % TM_POST == 0
    assert TM_IN % RET_L == 0 and seq % SB_T == 0 and RET_L % CHUNK == 0
    t = bsz * seq

    inv_freq = ROPE_BASE ** (-jnp.arange(0, RET_DQK, 2, dtype=F32) / RET_DQK)
    log_gamma = jnp.log1p(-(2.0 ** (-5.0 - jnp.arange(RET_HEADS, dtype=F32))))
    cos, sin = _rope_tables(positions, inv_freq)

    h_res = x.astype(F32).reshape(t, d)
    for l in range(depth):
        mod3 = _ada(c.astype(F32), ada_w[l], ada_b[l]).reshape(bsz, 6, d)
        ret, q_s, k_s, v_s, a_r, a_s = _inproj(
            log_gamma, h_res, mod3, cos, sin, pre_mix_g[l].reshape(1, d),
            ret_gn_g[l].reshape(1, RET_V), w_in[l].astype(BF16), seq)
        sb = _stick_breaking(q_s, k_s, v_s, bsz, seq)
        h_res = _post(h_res, mod3, ret, sb, a_r, a_s,
                      post_mix_g[l].reshape(1, d), pre_ffn_g[l].reshape(1, d),
                      post_ffn_g[l].reshape(1, d),
                      w_ret_branch[l].astype(BF16), w_sb_branch[l].astype(BF16),
                      w_out[l].astype(BF16), w_ff1[l].astype(BF16), w_ff2[l].astype(BF16), seq)
    return h_res.reshape(bsz, seq, d).astype(x.dtype)
```

```python
import functools

import jax
import jax.numpy as jnp
from jax import lax
from jax.experimental import pallas as pl
from jax.experimental.pallas import tpu as pltpu

D_MODEL = 1024
CHUNK = 64
RET_HEADS = 8
RET_DQK = 64
RET_DV = 128
SB_HEADS = 8
SB_DH = 64
D_FF = 4 * D_MODEL
ROPE_BASE = 10000.0
EPS = 1e-6

RET_QK = RET_HEADS * RET_DQK
RET_V = RET_HEADS * RET_DV
SB_W = SB_HEADS * SB_DH
D_IN = 2 * RET_QK + 2 * RET_V + 3 * SB_W + 2 * D_MODEL

LANES = 128
VMEM_LIMIT = 56 * 1024 * 1024

TM_IN = 512
TM_POST = 512
POST_SPLIT = 2
POST_FC = 1024
RET_L = 256
SB_T = 256
SB_SKIP_LOG = 105.0
SB_SKEW = 1
LOG2E = 1.4426950408889634

F32 = jnp.float32
BF16 = jnp.bfloat16


def _dot(a, b):
    return jnp.dot(a, b, preferred_element_type=F32)


def _dot_nt(a, b):
    return lax.dot_general(a, b, (((1,), (1,)), ((), ())), preferred_element_type=F32)


def _dot_tn(a, b):
    return lax.dot_general(a, b, (((0,), (0,)), ((), ())), preferred_element_type=F32)


def _rms(x, g):
    return x * lax.rsqrt(jnp.mean(x * x, axis=-1, keepdims=True) + EPS) * g


def _resident(shape):
    nd = len(shape)
    return pl.BlockSpec(shape, lambda *_: (0,) * nd, pipeline_mode=pl.Buffered(1))


def _ada_kernel(c_ref, w_ref, b_ref, o_ref):
    c = c_ref[...]
    s = (c * jax.nn.sigmoid(c)).astype(BF16)
    o_ref[...] = _dot(s, w_ref[...].astype(BF16)) + b_ref[...]


def _ada(c, w, b):
    bsz, d = c.shape
    n = w.shape[1]
    tn = 1024
    return pl.pallas_call(
        _ada_kernel,
        out_shape=jax.ShapeDtypeStruct((bsz, n), F32),
        grid=(n // tn,),
        in_specs=[pl.BlockSpec((bsz, d), lambda j: (0, 0)),
                  pl.BlockSpec((d, tn), lambda j: (0, j)),
                  pl.BlockSpec((1, tn), lambda j: (0, j))],
        out_specs=pl.BlockSpec((bsz, tn), lambda j: (0, j)),
        compiler_params=pltpu.CompilerParams(dimension_semantics=("arbitrary",)),
        name="ada",
    )(c, w, b.reshape(1, n))


def _rope_kernel(pos_ref, freq_ref, cos_ref, sin_ref):
    ang = freq_ref[...] * pos_ref[0].astype(F32)
    cos = jnp.cos(ang)
    sin = jnp.sin(ang)
    pairs = LANES // RET_DQK
    cos_ref[...] = jnp.concatenate([cos, cos] * pairs, axis=0).T
    sin_ref[...] = jnp.concatenate([-sin, sin] * pairs, axis=0).T


def _rope_tables(positions, inv_freq):
    bsz, seq = positions.shape
    nf = inv_freq.shape[0]
    return pl.pallas_call(
        _rope_kernel,
        out_shape=[jax.ShapeDtypeStruct((bsz * seq, LANES), F32)] * 2,
        grid=(bsz,),
        in_specs=[pl.BlockSpec((1, 1, seq), lambda b: (b, 0, 0)),
                  pl.BlockSpec((nf, 1), lambda b: (0, 0))],
        out_specs=[pl.BlockSpec((seq, LANES), lambda b: (b, 0))] * 2,
        compiler_params=pltpu.CompilerParams(dimension_semantics=("arbitrary",)),
        name="rope",
    )(positions.reshape(bsz, 1, seq), inv_freq.reshape(nf, 1))


def _inproj_kernel(lg_ref, x_ref, mod_ref, cos_ref, sin_ref, g_ref, gn_ref, w_ref,
                   ret_ref, qs_ref, ks_ref, vs_ref, ar_ref, as_ref,
                   qr_ref, kr_ref, vr_ref, gr_ref, state_ref, dtab_ref, qdec_ref, kdec_ref,
                   *, tiles_per_seq):
    i = pl.program_id(0)
    blk = RET_L

    @pl.when(i == 0)
    def _():
        ri = lax.broadcasted_iota(jnp.int32, (blk, blk), 0)
        ci = lax.broadcasted_iota(jnp.int32, (blk, blk), 1)
        dist = jnp.abs(ri - ci).astype(F32)
        keep = (ci // CHUNK) <= (ri // CHUNK)
        idx = lax.broadcasted_iota(jnp.int32, (blk, LANES), 0).astype(F32)
        for hd in range(RET_HEADS):
            dtab_ref[hd] = jnp.where(keep, jnp.exp(dist * lg_ref[hd]), 0.0)
            qdec_ref[hd] = jnp.exp((idx + 1.0) * lg_ref[hd])
            kdec_ref[hd] = jnp.exp((blk - 1.0 - idx) * lg_ref[hd])

    @pl.when(lax.rem(i, tiles_per_seq) == 0)
    def _():
        state_ref[...] = jnp.zeros_like(state_ref)

    h = _rms(x_ref[...], g_ref[...])
    h = h * (1.0 + mod_ref[0, 1:2, :]) + mod_ref[0, 0:1, :]
    hb = h.astype(BF16)

    cos = cos_ref[...]
    sin = sin_ref[...]
    lane = lax.broadcasted_iota(jnp.int32, (1, LANES), 1)
    first_half = (lane % RET_DQK) < (RET_DQK // 2)

    def rope(p):
        rot = jnp.where(first_half, pltpu.roll(p, LANES - RET_DQK // 2, 1),
                        pltpu.roll(p, RET_DQK // 2, 1))
        return p * cos + rot * sin

    def proj(c0, width):
        return _dot(hb, w_ref[:, c0:c0 + width])

    c0 = 0
    acc = proj(c0, RET_QK)
    for j in range(RET_QK // LANES):
        qr_ref[:, j * LANES:(j + 1) * LANES] = rope(acc[:, j * LANES:(j + 1) * LANES]).astype(BF16)
    c0 += RET_QK
    acc = proj(c0, RET_QK)
    kscale = RET_DQK ** -0.5
    for j in range(RET_QK // LANES):
        kr_ref[:, j * LANES:(j + 1) * LANES] = (
            rope(acc[:, j * LANES:(j + 1) * LANES]) * kscale).astype(BF16)
    c0 += RET_QK
    for ref, width in ((vr_ref, RET_V), (gr_ref, RET_V)):
        ref[...] = proj(c0, width).astype(BF16)
        c0 += width

    def retention_block(rows):
        for hd in range(RET_HEADS):
            p = hd // 2
            in_head = (lane // RET_DQK) == (hd % 2)
            qp = qr_ref[rows, p * LANES:(p + 1) * LANES]
            kp = kr_ref[rows, p * LANES:(p + 1) * LANES]
            vh = vr_ref[rows, hd * RET_DV:(hd + 1) * RET_DV]
            qh = jnp.where(in_head, qp, jnp.zeros_like(qp))
            st = state_ref[hd]

            scores = _dot_nt(qh, kp) * dtab_ref[hd]
            intra = _dot(scores.astype(BF16), vh)
            cross = _dot(qh, st.astype(BF16)) * qdec_ref[hd]
            kd = (kp.astype(F32) * kdec_ref[hd]).astype(BF16)
            state_ref[hd] = st * jnp.exp(blk * lg_ref[hd]) + _dot_tn(kd, vh)

            o = intra + cross
            mu = jnp.mean(o, axis=-1, keepdims=True)
            dlt = o - mu
            var = jnp.mean(dlt * dlt, axis=-1, keepdims=True)
            y = dlt * lax.rsqrt(var + EPS) * gn_ref[:, hd * RET_DV:(hd + 1) * RET_DV]
            g = gr_ref[rows, hd * RET_DV:(hd + 1) * RET_DV].astype(F32)
            ret_ref[rows, hd * RET_DV:(hd + 1) * RET_DV] = (g * jax.nn.sigmoid(g) * y).astype(BF16)

    rest = ((qs_ref, SB_W), (ks_ref, SB_W), (vs_ref, SB_W), (ar_ref, D_MODEL), (as_ref, D_MODEL))
    nblk = x_ref.shape[0] // blk
    assert nblk <= len(rest)
    for n, (ref, width) in enumerate(rest):
        if n < nblk:
            retention_block(pl.ds(n * blk, blk))
        ref[...] = proj(c0, width).astype(BF16)
        c0 += width


def _inproj(log_gamma, x2, mod3, cos, sin, g, gn, w_bf, seq):
    t, d = x2.shape
    tm = TM_IN
    tiles_per_seq = seq // tm
    widths = (RET_V, SB_W, SB_W, SB_W, D_MODEL, D_MODEL)
    row = lambda i: (i, 0)
    return pl.pallas_call(
        functools.partial(_inproj_kernel, tiles_per_seq=tiles_per_seq),
        out_shape=[jax.ShapeDtypeStruct((t, w), BF16) for w in widths],
        grid=(t // tm,),
        in_specs=[pl.BlockSpec(memory_space=pltpu.SMEM),
                  pl.BlockSpec((tm, d), row),
                  pl.BlockSpec((1, 6, d), lambda i: (i // tiles_per_seq, 0, 0)),
                  pl.BlockSpec((tm, LANES), row),
                  pl.BlockSpec((tm, LANES), row),
                  _resident((1, d)),
                  _resident((1, RET_V)),
                  _resident((d, D_IN))],
        out_specs=[pl.BlockSpec((tm, w), row) for w in widths],
        scratch_shapes=[pltpu.VMEM((tm, RET_QK), BF16), pltpu.VMEM((tm, RET_QK), BF16),
                        pltpu.VMEM((tm, RET_V), BF16), pltpu.VMEM((tm, RET_V), BF16),
                        pltpu.VMEM((RET_HEADS, LANES, RET_DV), F32),
                        pltpu.VMEM((RET_HEADS, RET_L, RET_L), F32),
                        pltpu.VMEM((RET_HEADS, RET_L, LANES), F32),
                        pltpu.VMEM((RET_HEADS, RET_L, LANES), F32)],
        compiler_params=pltpu.CompilerParams(dimension_semantics=("arbitrary",),
                                             vmem_limit_bytes=VMEM_LIMIT),
        name="inproj",
    )(log_gamma, x2, mod3, cos, sin, g, gn, w_bf)


def _sb_kernel(q_ref, k_ref, v_ref, o_ref, qm_ref, tri_ref, acc_ref, r_ref):
    qi = pl.program_id(1)
    t = SB_T
    lane = lax.broadcasted_iota(jnp.int32, (1, LANES), 1)

    @pl.when(jnp.logical_and(pl.program_id(0) == 0, qi == 0))
    def _():
        ri = lax.broadcasted_iota(jnp.int32, (t, t), 0)
        ci = lax.broadcasted_iota(jnp.int32, (t, t), 1)
        tri_ref[...] = jnp.where(ri > ci, 1.0, 0.0).astype(BF16)

    for h in range(SB_HEADS):
        qp = q_ref[:, (h // 2) * LANES:(h // 2 + 1) * LANES]
        in_head = (lane // SB_DH) == (h % 2)
        qm_ref[h] = jnp.where(in_head, qp, jnp.zeros_like(qp)) * (SB_DH ** -0.5)

    ri = lax.broadcasted_iota(jnp.int32, (t, t), 0)
    ci = lax.broadcasted_iota(jnp.int32, (t, t), 1)
    valid = ci < ri

    def stage_scores(h, start, diag):
        cols = slice((h // 2) * LANES, (h // 2 + 1) * LANES)
        z = _dot_nt(qm_ref[h], k_ref[pl.ds(start, t), cols])
        sp = jnp.maximum(z, 0.0) + jnp.log(1.0 + jnp.exp2(jnp.abs(z) * (-LOG2E)))
        d = z - sp
        if diag:
            sp = jnp.where(valid, sp, 0.0)
        else:
            d = d - jnp.concatenate([r_ref[h]] * (t // LANES), axis=-1)
        rs = jnp.broadcast_to(jnp.sum(sp, axis=-1, keepdims=True), (t, LANES))
        if diag:
            r_ref[h] = rs
        else:
            r_ref[h] += rs
        return sp.astype(BF16), d

    def stage_weights(spb, d, diag):
        a = jnp.exp(d - _dot(spb, tri_ref[...]))
        if diag:
            a = jnp.where(valid, a, 0.0)
        return a.astype(BF16)

    def stage_values(h, ab, start, diag):
        cols = slice((h // 2) * LANES, (h // 2 + 1) * LANES)
        pv = _dot(ab, v_ref[pl.ds(start, t), cols])
        if diag:
            acc_ref[h] = pv
        else:
            acc_ref[h] += pv

    def sweep(blocks):
        items = [(h, start, diag) for start, diag in blocks for h in range(SB_HEADS)]
        s1 = {}
        s2 = {}
        for step in range(len(items) + 2 * SB_SKEW):
            if step < len(items):
                h, start, diag = items[step]
                s1[step] = stage_scores(h, start, diag)
            n = step - SB_SKEW
            if 0 <= n < len(items):
                s2[n] = stage_weights(*s1.pop(n), items[n][2])
            n = step - 2 * SB_SKEW
            if 0 <= n < len(items):
                h, start, diag = items[n]
                stage_values(h, s2.pop(n), start, diag)

    def key_block(it):
        return pl.multiple_of((qi - 1 - it) * t, t)

    diag_block = (pl.multiple_of(qi * t, t), True)

    @pl.when(qi == 0)
    def _():
        sweep([diag_block])

    @pl.when(qi > 0)
    def _():
        sweep([diag_block, (key_block(0), False)])

    def all_rows_done():
        r_min = r_ref[0]
        for h in range(1, SB_HEADS):
            r_min = jnp.minimum(r_min, r_ref[h])
        return jnp.min(r_min) >= SB_SKIP_LOG

    def cond(carry):
        it, done = carry
        return jnp.logical_and(it < qi, jnp.logical_not(done))

    def body(carry):
        it, _ = carry
        sweep([(key_block(it), False)])
        return it + 1, all_rows_done()

    lax.while_loop(cond, body, (jnp.int32(1), all_rows_done()))

    for p in range(SB_HEADS // 2):
        o_ref[:, p * LANES:(p + 1) * LANES] = jnp.where(
            lane < SB_DH, acc_ref[2 * p], acc_ref[2 * p + 1]).astype(BF16)


def _stick_breaking(q, k, v, bsz, seq):
    t = q.shape[0]
    nq = seq // SB_T
    return pl.pallas_call(
        _sb_kernel,
        out_shape=jax.ShapeDtypeStruct((t, SB_W), BF16),
        grid=(bsz, nq),
        in_specs=[pl.BlockSpec((SB_T, SB_W), lambda b, i: (b * nq + i, 0)),
                  pl.BlockSpec((seq, SB_W), lambda b, i: (b, 0)),
                  pl.BlockSpec((seq, SB_W), lambda b, i: (b, 0))],
        out_specs=pl.BlockSpec((SB_T, SB_W), lambda b, i: (b * nq + i, 0)),
        scratch_shapes=[pltpu.VMEM((SB_HEADS, SB_T, LANES), BF16),
                        pltpu.VMEM((SB_T, SB_T), BF16),
                        pltpu.VMEM((SB_HEADS, SB_T, LANES), F32),
                        pltpu.VMEM((SB_HEADS, SB_T, LANES), F32)],
        compiler_params=pltpu.CompilerParams(dimension_semantics=("arbitrary", "arbitrary"),
                                             vmem_limit_bytes=VMEM_LIMIT),
        name="stickbreak",
    )(q, k, v)


def _post_kernel(x_ref, mod_ref, ret_ref, sb_ref, ar_ref, as_ref,
                 g_mix_ref, g_pre_ref, g_post_ref,
                 w_ret_ref, w_sb_ref, w_out_ref, w_ff1_ref, w_ff2_ref, o_ref):
    gt1 = mod_ref[0, 2:3, :]
    sh2 = mod_ref[0, 3:4, :]
    sc2 = mod_ref[0, 4:5, :]
    gt2 = mod_ref[0, 5:6, :]

    def mix(rows):
        mixed = (jax.nn.sigmoid(ar_ref[rows, :].astype(F32)) * _dot(ret_ref[rows, :], w_ret_ref[...])
                 + jax.nn.sigmoid(as_ref[rows, :].astype(F32)) * _dot(sb_ref[rows, :], w_sb_ref[...]))
        y = _dot(mixed.astype(BF16), w_out_ref[...])
        h_res = x_ref[rows, :] + gt1 * _rms(y, g_mix_ref[...])
        h2 = (_rms(h_res, g_pre_ref[...]) * (1.0 + sc2) + sh2).astype(BF16)
        return h_res, h2

    def mlp(h2):
        f = None
        fc = POST_FC
        for c in range(D_FF // fc):
            u = jnp.maximum(_dot(h2, w_ff1_ref[:, c * fc:(c + 1) * fc]), 0.0)
            part = _dot((u * u).astype(BF16), w_ff2_ref[c * fc:(c + 1) * fc, :])
            f = part if f is None else f + part
        return f

    tm = x_ref.shape[0]
    groups = [pl.ds(i * (tm // POST_SPLIT), tm // POST_SPLIT) for i in range(POST_SPLIT)]
    mixed = [mix(rows) for rows in groups]
    for rows, (h_res, h2) in zip(groups, mixed):
        o_ref[rows, :] = h_res + gt2 * _rms(mlp(h2), g_post_ref[...])


def _post(x2, mod3, ret, sb, a_r, a_s, g_mix, g_pre, g_post, w_ret, w_sb, w_out, w_ff1, w_ff2, seq):
    t, d = x2.shape
    tm = TM_POST
    tiles_per_seq = seq // tm
    row = lambda i: (i, 0)
    return pl.pallas_call(
        _post_kernel,
        out_shape=jax.ShapeDtypeStruct((t, d), F32),
        grid=(t // tm,),
        in_specs=[pl.BlockSpec((tm, d), row),
                  pl.BlockSpec((1, 6, d), lambda i: (i // tiles_per_seq, 0, 0)),
                  pl.BlockSpec((tm, RET_V), row),
                  pl.BlockSpec((tm, SB_W), row),
                  pl.BlockSpec((tm, d), row),
                  pl.BlockSpec((tm, d), row),
                  _resident((1, d)), _resident((1, d)), _resident((1, d)),
                  _resident((RET_V, d)), _resident((SB_W, d)), _resident((d, d)),
                  _resident((d, D_FF)), _resident((D_FF, d))],
        out_specs=pl.BlockSpec((tm, d), row),
        compiler_params=pltpu.CompilerParams(dimension_semantics=("arbitrary",),
                                             vmem_limit_bytes=VMEM_LIMIT),
        name="post",
    )(x2, mod3, ret, sb, a_r, a_s, g_mix, g_pre, g_post, w_ret, w_sb, w_out, w_ff1, w_ff2)


def kernel(x, c, positions, ada_w, ada_b, pre_mix_g, post_mix_g, pre_ffn_g, post_ffn_g,
           w_in, ret_gn_g, w_ret_branch, w_sb_branch, w_out, w_ff1, w_ff2):
    bsz, seq, d = x.shape
    depth = ada_w.shape[0]
    assert d == D_MODEL and seq % TM_IN == 0 and seq % TM_POST == 0
    assert TM_IN % RET_L == 0 and seq % SB_T == 0 and RET_L % CHUNK == 0
    t = bsz * seq

    inv_freq = ROPE_BASE ** (-jnp.arange(0, RET_DQK, 2, dtype=F32) / RET_DQK)
    log_gamma = jnp.log1p(-(2.0 ** (-5.0 - jnp.arange(RET_HEADS, dtype=F32))))
    cos, sin = _rope_tables(positions, inv_freq)

    h_res = x.astype(F32).reshape(t, d)
    for l in range(depth):
        mod3 = _ada(c.astype(F32), ada_w[l], ada_b[l]).reshape(bsz, 6, d)
        ret, q_s, k_s, v_s, a_r, a_s = _inproj(
            log_gamma, h_res, mod3, cos, sin, pre_mix_g[l].reshape(1, d),
            ret_gn_g[l].reshape(1, RET_V), w_in[l].astype(BF16), seq)
        sb = _stick_breaking(q_s, k_s, v_s, bsz, seq)
        h_res = _post(h_res, mod3, ret, sb, a_r, a_s,
                      post_mix_g[l].reshape(1, d), pre_ffn_g[l].reshape(1, d),
                      post_ffn_g[l].reshape(1, d),
                      w_ret_branch[l].astype(BF16), w_sb_branch[l].astype(BF16),
                      w_out[l].astype(BF16), w_ff1[l].astype(BF16), w_ff2[l].astype(BF16), seq)
    return h_res.reshape(bsz, seq, d).astype(x.dtype)
```

```python
import functools

import jax
import jax.numpy as jnp
from jax import lax
from jax.experimental import pallas as pl
from jax.experimental.pallas import tpu as pltpu

D_MODEL = 1024
CHUNK = 64
RET_HEADS = 8
RET_DQK = 64
RET_DV = 128
SB_HEADS = 8
SB_DH = 64
D_FF = 4 * D_MODEL
ROPE_BASE = 10000.0
EPS = 1e-6

RET_QK = RET_HEADS * RET_DQK
RET_V = RET_HEADS * RET_DV
SB_W = SB_HEADS * SB_DH
D_IN = 2 * RET_QK + 2 * RET_V + 3 * SB_W + 2 * D_MODEL

LANES = 128
BF16_ROWS = 16
N_LATE_W = 5
VMEM_LIMIT = 56 * 1024 * 1024

TM_IN = 512
TM_POST = 512
POST_SPLIT = 2
POST_FC = 1024
RET_L = 256
SB_T = 256
SB_SKIP_LOG = 105.0
SB_SKEW = 1
LOG2E = 1.4426950408889634

F32 = jnp.float32
BF16 = jnp.bfloat16


def _dot(a, b):
    return jnp.dot(a, b, preferred_element_type=F32)


def _dot_nt(a, b):
    return lax.dot_general(a, b, (((1,), (1,)), ((), ())), preferred_element_type=F32)


def _dot_tn(a, b):
    return lax.dot_general(a, b, (((0,), (0,)), ((), ())), preferred_element_type=F32)


def _rms(x, g):
    return x * lax.rsqrt(jnp.mean(x * x, axis=-1, keepdims=True) + EPS) * g


def _resident(shape):
    nd = len(shape)
    return pl.BlockSpec(shape, lambda *_: (0,) * nd, pipeline_mode=pl.Buffered(1))


def _ada_kernel(c_ref, w_ref, b_ref, o_ref):
    c = c_ref[...]
    s = (c * jax.nn.sigmoid(c)).astype(BF16)
    o_ref[...] = _dot(s, w_ref[...].astype(BF16)) + b_ref[...]


def _ada(c, w, b):
    bsz, d = c.shape
    n = w.shape[1]
    tn = 1024
    return pl.pallas_call(
        _ada_kernel,
        out_shape=jax.ShapeDtypeStruct((bsz, n), F32),
        grid=(n // tn,),
        in_specs=[pl.BlockSpec((bsz, d), lambda j: (0, 0)),
                  pl.BlockSpec((d, tn), lambda j: (0, j)),
                  pl.BlockSpec((1, tn), lambda j: (0, j))],
        out_specs=pl.BlockSpec((bsz, tn), lambda j: (0, j)),
        compiler_params=pltpu.CompilerParams(dimension_semantics=("arbitrary",)),
        name="ada",
    )(c, w, b.reshape(1, n))


def _rope_kernel(pos_ref, freq_ref, w_ref, cos_ref, sin_ref, wb_ref):
    ang = freq_ref[...] * pos_ref[0].astype(F32)
    cos = jnp.cos(ang)
    sin = jnp.sin(ang)
    pairs = LANES // RET_DQK
    cos_ref[...] = jnp.concatenate([cos, cos] * pairs, axis=0).T
    sin_ref[...] = jnp.concatenate([-sin, sin] * pairs, axis=0).T
    wb_ref[...] = w_ref[...].astype(BF16)


def _rope_tables(positions, inv_freq, w):
    bsz, seq = positions.shape
    nf = inv_freq.shape[0]
    d, n = w.shape
    rows = d // bsz
    assert d % bsz == 0 and rows % BF16_ROWS == 0
    return pl.pallas_call(
        _rope_kernel,
        out_shape=[jax.ShapeDtypeStruct((bsz * seq, LANES), F32)] * 2
        + [jax.ShapeDtypeStruct((d, n), BF16)],
        grid=(bsz,),
        in_specs=[pl.BlockSpec((1, 1, seq), lambda b: (b, 0, 0)),
                  pl.BlockSpec((nf, 1), lambda b: (0, 0)),
                  pl.BlockSpec((rows, n), lambda b: (b, 0))],
        out_specs=[pl.BlockSpec((seq, LANES), lambda b: (b, 0))] * 2
        + [pl.BlockSpec((rows, n), lambda b: (b, 0))],
        compiler_params=pltpu.CompilerParams(dimension_semantics=("arbitrary",)),
        name="rope",
    )(positions.reshape(bsz, 1, seq), inv_freq.reshape(nf, 1), w)


def _inproj_kernel(lg_ref, x_ref, mod_ref, cos_ref, sin_ref, g_ref, gn_ref, w_ref, *refs,
                   tiles_per_seq):
    late_w = refs[:N_LATE_W]
    ret_ref, qs_ref, ks_ref, vs_ref, ar_ref, as_ref = refs[N_LATE_W:N_LATE_W + 6]
    late_wb = refs[N_LATE_W + 6:2 * N_LATE_W + 6]
    (qr_ref, kr_ref, vr_ref, gr_ref,
     state_ref, dtab_ref, qdec_ref, kdec_ref) = refs[2 * N_LATE_W + 6:]
    i = pl.program_id(0)
    blk = RET_L

    for src, dst in zip(late_w, late_wb):
        dst[...] = src[...].astype(BF16)

    @pl.when(i == 0)
    def _():
        ri = lax.broadcasted_iota(jnp.int32, (blk, blk), 0)
        ci = lax.broadcasted_iota(jnp.int32, (blk, blk), 1)
        dist = jnp.abs(ri - ci).astype(F32)
        keep = (ci // CHUNK) <= (ri // CHUNK)
        idx = lax.broadcasted_iota(jnp.int32, (blk, LANES), 0).astype(F32)
        for hd in range(RET_HEADS):
            dtab_ref[hd] = jnp.where(keep, jnp.exp(dist * lg_ref[hd]), 0.0)
            qdec_ref[hd] = jnp.exp((idx + 1.0) * lg_ref[hd])
            kdec_ref[hd] = jnp.exp((blk - 1.0 - idx) * lg_ref[hd])

    @pl.when(lax.rem(i, tiles_per_seq) == 0)
    def _():
        state_ref[...] = jnp.zeros_like(state_ref)

    h = _rms(x_ref[...], g_ref[...])
    h = h * (1.0 + mod_ref[0, 1:2, :]) + mod_ref[0, 0:1, :]
    hb = h.astype(BF16)

    cos = cos_ref[...]
    sin = sin_ref[...]
    lane = lax.broadcasted_iota(jnp.int32, (1, LANES), 1)
    first_half = (lane % RET_DQK) < (RET_DQK // 2)

    def rope(p):
        rot = jnp.where(first_half, pltpu.roll(p, LANES - RET_DQK // 2, 1),
                        pltpu.roll(p, RET_DQK // 2, 1))
        return p * cos + rot * sin

    def proj(c0, width):
        return _dot(hb, w_ref[:, c0:c0 + width])

    c0 = 0
    acc = proj(c0, RET_QK)
    for j in range(RET_QK // LANES):
        qr_ref[:, j * LANES:(j + 1) * LANES] = rope(acc[:, j * LANES:(j + 1) * LANES]).astype(BF16)
    c0 += RET_QK
    acc = proj(c0, RET_QK)
    kscale = RET_DQK ** -0.5
    for j in range(RET_QK // LANES):
        kr_ref[:, j * LANES:(j + 1) * LANES] = (
            rope(acc[:, j * LANES:(j + 1) * LANES]) * kscale).astype(BF16)
    c0 += RET_QK
    for ref, width in ((vr_ref, RET_V), (gr_ref, RET_V)):
        ref[...] = proj(c0, width).astype(BF16)
        c0 += width

    def retention_block(rows):
        for hd in range(RET_HEADS):
            p = hd // 2
            in_head = (lane // RET_DQK) == (hd % 2)
            qp = qr_ref[rows, p * LANES:(p + 1) * LANES]
            kp = kr_ref[rows, p * LANES:(p + 1) * LANES]
            vh = vr_ref[rows, hd * RET_DV:(hd + 1) * RET_DV]
            qh = jnp.where(in_head, qp, jnp.zeros_like(qp))
            st = state_ref[hd]

            scores = _dot_nt(qh, kp) * dtab_ref[hd]
            intra = _dot(scores.astype(BF16), vh)
            cross = _dot(qh, st.astype(BF16)) * qdec_ref[hd]
            kd = (kp.astype(F32) * kdec_ref[hd]).astype(BF16)
            state_ref[hd] = st * jnp.exp(blk * lg_ref[hd]) + _dot_tn(kd, vh)

            o = intra + cross
            mu = jnp.mean(o, axis=-1, keepdims=True)
            dlt = o - mu
            var = jnp.mean(dlt * dlt, axis=-1, keepdims=True)
            y = dlt * lax.rsqrt(var + EPS) * gn_ref[:, hd * RET_DV:(hd + 1) * RET_DV]
            g = gr_ref[rows, hd * RET_DV:(hd + 1) * RET_DV].astype(F32)
            ret_ref[rows, hd * RET_DV:(hd + 1) * RET_DV] = (g * jax.nn.sigmoid(g) * y).astype(BF16)

    rest = ((qs_ref, SB_W), (ks_ref, SB_W), (vs_ref, SB_W), (ar_ref, D_MODEL), (as_ref, D_MODEL))
    nblk = x_ref.shape[0] // blk
    assert nblk <= len(rest)
    for n, (ref, width) in enumerate(rest):
        if n < nblk:
            retention_block(pl.ds(n * blk, blk))
        ref[...] = proj(c0, width).astype(BF16)
        c0 += width


def _inproj(log_gamma, x2, mod3, cos, sin, g, gn, w_bf, late_w, seq):
    t, d = x2.shape
    tm = TM_IN
    steps = t // tm
    tiles_per_seq = seq // tm
    widths = (RET_V, SB_W, SB_W, SB_W, D_MODEL, D_MODEL)
    row = lambda i: (i, 0)
    assert len(late_w) == N_LATE_W
    slabs = []
    for w in late_w:
        assert w.shape[0] % (steps * BF16_ROWS) == 0
        slabs.append(pl.BlockSpec((w.shape[0] // steps, w.shape[1]), row))
    return pl.pallas_call(
        functools.partial(_inproj_kernel, tiles_per_seq=tiles_per_seq),
        out_shape=[jax.ShapeDtypeStruct((t, w), BF16) for w in widths]
        + [jax.ShapeDtypeStruct(w.shape, BF16) for w in late_w],
        grid=(steps,),
        in_specs=[pl.BlockSpec(memory_space=pltpu.SMEM),
                  pl.BlockSpec((tm, d), row),
                  pl.BlockSpec((1, 6, d), lambda i: (i // tiles_per_seq, 0, 0)),
                  pl.BlockSpec((tm, LANES), row),
                  pl.BlockSpec((tm, LANES), row),
                  _resident((1, d)),
                  _resident((1, RET_V)),
                  _resident((d, D_IN))] + slabs,
        out_specs=[pl.BlockSpec((tm, w), row) for w in widths] + slabs,
        scratch_shapes=[pltpu.VMEM((tm, RET_QK), BF16), pltpu.VMEM((tm, RET_QK), BF16),
                        pltpu.VMEM((tm, RET_V), BF16), pltpu.VMEM((tm, RET_V), BF16),
                        pltpu.VMEM((RET_HEADS, LANES, RET_DV), F32),
                        pltpu.VMEM((RET_HEADS, RET_L, RET_L), F32),
                        pltpu.VMEM((RET_HEADS, RET_L, LANES), F32),
                        pltpu.VMEM((RET_HEADS, RET_L, LANES), F32)],
        compiler_params=pltpu.CompilerParams(dimension_semantics=("arbitrary",),
                                             vmem_limit_bytes=VMEM_LIMIT),
        name="inproj",
    )(log_gamma, x2, mod3, cos, sin, g, gn, w_bf, *late_w)


def _sb_kernel(q_ref, k_ref, v_ref, o_ref, qm_ref, tri_ref, acc_ref, r_ref):
    qi = pl.program_id(1)
    t = SB_T
    lane = lax.broadcasted_iota(jnp.int32, (1, LANES), 1)

    @pl.when(jnp.logical_and(pl.program_id(0) == 0, qi == 0))
    def _():
        ri = lax.broadcasted_iota(jnp.int32, (t, t), 0)
        ci = lax.broadcasted_iota(jnp.int32, (t, t), 1)
        tri_ref[...] = jnp.where(ri > ci, 1.0, 0.0).astype(BF16)

    for h in range(SB_HEADS):
        qp = q_ref[:, (h // 2) * LANES:(h // 2 + 1) * LANES]
        in_head = (lane // SB_DH) == (h % 2)
        qm_ref[h] = jnp.where(in_head, qp, jnp.zeros_like(qp)) * (SB_DH ** -0.5)

    ri = lax.broadcasted_iota(jnp.int32, (t, t), 0)
    ci = lax.broadcasted_iota(jnp.int32, (t, t), 1)
    valid = ci < ri

    def stage_scores(h, start, diag):
        cols = slice((h // 2) * LANES, (h // 2 + 1) * LANES)
        z = _dot_nt(qm_ref[h], k_ref[pl.ds(start, t), cols])
        sp = jnp.maximum(z, 0.0) + jnp.log(1.0 + jnp.exp2(jnp.abs(z) * (-LOG2E)))
        d = z - sp
        if diag:
            sp = jnp.where(valid, sp, 0.0)
        else:
            d = d - jnp.concatenate([r_ref[h]] * (t // LANES), axis=-1)
        rs = jnp.broadcast_to(jnp.sum(sp, axis=-1, keepdims=True), (t, LANES))
        if diag:
            r_ref[h] = rs
        else:
            r_ref[h] += rs
        return sp.astype(BF16), d

    def stage_weights(spb, d, diag):
        a = jnp.exp(d - _dot(spb, tri_ref[...]))
        if diag:
            a = jnp.where(valid, a, 0.0)
        return a.astype(BF16)

    def stage_values(h, ab, start, diag):
        cols = slice((h // 2) * LANES, (h // 2 + 1) * LANES)
        pv = _dot(ab, v_ref[pl.ds(start, t), cols])
        if diag:
            acc_ref[h] = pv
        else:
            acc_ref[h] += pv

    def sweep(blocks):
        items = [(h, start, diag) for start, diag in blocks for h in range(SB_HEADS)]
        s1 = {}
        s2 = {}
        for step in range(len(items) + 2 * SB_SKEW):
            if step < len(items):
                h, start, diag = items[step]
                s1[step] = stage_scores(h, start, diag)
            n = step - SB_SKEW
            if 0 <= n < len(items):
                s2[n] = stage_weights(*s1.pop(n), items[n][2])
            n = step - 2 * SB_SKEW
            if 0 <= n < len(items):
                h, start, diag = items[n]
                stage_values(h, s2.pop(n), start, diag)

    def key_block(it):
        return pl.multiple_of((qi - 1 - it) * t, t)

    diag_block = (pl.multiple_of(qi * t, t), True)

    @pl.when(qi == 0)
    def _():
        sweep([diag_block])

    @pl.when(qi > 0)
    def _():
        sweep([diag_block, (key_block(0), False)])

    def all_rows_done():
        r_min = r_ref[0]
        for h in range(1, SB_HEADS):
            r_min = jnp.minimum(r_min, r_ref[h])
        return jnp.min(r_min) >= SB_SKIP_LOG

    def cond(carry):
        it, done = carry
        return jnp.logical_and(it < qi, jnp.logical_not(done))

    def body(carry):
        it, _ = carry
        sweep([(key_block(it), False)])
        return it + 1, all_rows_done()

    lax.while_loop(cond, body, (jnp.int32(1), all_rows_done()))

    for p in range(SB_HEADS // 2):
        o_ref[:, p * LANES:(p + 1) * LANES] = jnp.where(
            lane < SB_DH, acc_ref[2 * p], acc_ref[2 * p + 1]).astype(BF16)


def _stick_breaking(q, k, v, bsz, seq):
    t = q.shape[0]
    nq = seq // SB_T
    return pl.pallas_call(
        _sb_kernel,
        out_shape=jax.ShapeDtypeStruct((t, SB_W), BF16),
        grid=(bsz, nq),
        in_specs=[pl.BlockSpec((SB_T, SB_W), lambda b, i: (b * nq + i, 0)),
                  pl.BlockSpec((seq, SB_W), lambda b, i: (b, 0)),
                  pl.BlockSpec((seq, SB_W), lambda b, i: (b, 0))],
        out_specs=pl.BlockSpec((SB_T, SB_W), lambda b, i: (b * nq + i, 0)),
        scratch_shapes=[pltpu.VMEM((SB_HEADS, SB_T, LANES), BF16),
                        pltpu.VMEM((SB_T, SB_T), BF16),
                        pltpu.VMEM((SB_HEADS, SB_T, LANES), F32),
                        pltpu.VMEM((SB_HEADS, SB_T, LANES), F32)],
        compiler_params=pltpu.CompilerParams(dimension_semantics=("arbitrary", "arbitrary"),
                                             vmem_limit_bytes=VMEM_LIMIT),
        name="stickbreak",
    )(q, k, v)


def _post_kernel(x_ref, mod_ref, ret_ref, sb_ref, ar_ref, as_ref,
                 g_mix_ref, g_pre_ref, g_post_ref,
                 w_ret_ref, w_sb_ref, w_out_ref, w_ff1_ref, w_ff2_ref, o_ref):
    gt1 = mod_ref[0, 2:3, :]
    sh2 = mod_ref[0, 3:4, :]
    sc2 = mod_ref[0, 4:5, :]
    gt2 = mod_ref[0, 5:6, :]

    def mix(rows):
        mixed = (jax.nn.sigmoid(ar_ref[rows, :].astype(F32)) * _dot(ret_ref[rows, :], w_ret_ref[...])
                 + jax.nn.sigmoid(as_ref[rows, :].astype(F32)) * _dot(sb_ref[rows, :], w_sb_ref[...]))
        y = _dot(mixed.astype(BF16), w_out_ref[...])
        h_res = x_ref[rows, :] + gt1 * _rms(y, g_mix_ref[...])
        h2 = (_rms(h_res, g_pre_ref[...]) * (1.0 + sc2) + sh2).astype(BF16)
        return h_res, h2

    def mlp(h2):
        f = None
        fc = POST_FC
        for c in range(D_FF // fc):
            u = jnp.maximum(_dot(h2, w_ff1_ref[:, c * fc:(c + 1) * fc]), 0.0)
            part = _dot((u * u).astype(BF16), w_ff2_ref[c * fc:(c + 1) * fc, :])
            f = part if f is None else f + part
        return f

    tm = x_ref.shape[0]
    groups = [pl.ds(i * (tm // POST_SPLIT), tm // POST_SPLIT) for i in range(POST_SPLIT)]
    mixed = [mix(rows) for rows in groups]
    for rows, (h_res, h2) in zip(groups, mixed):
        o_ref[rows, :] = h_res + gt2 * _rms(mlp(h2), g_post_ref[...])


def _post(x2, mod3, ret, sb, a_r, a_s, g_mix, g_pre, g_post, w_ret, w_sb, w_out, w_ff1, w_ff2, seq):
    t, d = x2.shape
    tm = TM_POST
    tiles_per_seq = seq // tm
    row = lambda i: (i, 0)
    return pl.pallas_call(
        _post_kernel,
        out_shape=jax.ShapeDtypeStruct((t, d), F32),
        grid=(t // tm,),
        in_specs=[pl.BlockSpec((tm, d), row),
                  pl.BlockSpec((1, 6, d), lambda i: (i // tiles_per_seq, 0, 0)),
                  pl.BlockSpec((tm, RET_V), row),
                  pl.BlockSpec((tm, SB_W), row),
                  pl.BlockSpec((tm, d), row),
                  pl.BlockSpec((tm, d), row),
                  _resident((1, d)), _resident((1, d)), _resident((1, d)),
                  _resident((RET_V, d)), _resident((SB_W, d)), _resident((d, d)),
                  _resident((d, D_FF)), _resident((D_FF, d))],
        out_specs=pl.BlockSpec((tm, d), row),
        compiler_params=pltpu.CompilerParams(dimension_semantics=("arbitrary",),
                                             vmem_limit_bytes=VMEM_LIMIT),
        name="post",
    )(x2, mod3, ret, sb, a_r, a_s, g_mix, g_pre, g_post, w_ret, w_sb, w_out, w_ff1, w_ff2)


def kernel(x, c, positions, ada_w, ada_b, pre_mix_g, post_mix_g, pre_ffn_g, post_ffn_g,
           w_in, ret_gn_g, w_ret_branch, w_sb_branch, w_out, w_ff1, w_ff2):
    bsz, seq, d = x.shape
    depth = ada_w.shape[0]
    assert d == D_MODEL and seq % TM_IN == 0 and seq % TM_POST == 0
    assert TM_IN % RET_L == 0 and seq % SB_T == 0 and RET_L % CHUNK == 0
    t = bsz * seq

    inv_freq = ROPE_BASE ** (-jnp.arange(0, RET_DQK, 2, dtype=F32) / RET_DQK)
    log_gamma = jnp.log1p(-(2.0 ** (-5.0 - jnp.arange(RET_HEADS, dtype=F32))))
    cos, sin, w_in0 = _rope_tables(positions, inv_freq, w_in[0])

    h_res = x.astype(F32).reshape(t, d)
    for l in range(depth):
        mod3 = _ada(c.astype(F32), ada_w[l], ada_b[l]).reshape(bsz, 6, d)
        w_in_bf = w_in0 if l == 0 else w_in[l].astype(BF16)
        late_w = (w_ret_branch[l], w_sb_branch[l], w_out[l], w_ff1[l], w_ff2[l])
        ret, q_s, k_s, v_s, a_r, a_s, *late_wb = _inproj(
            log_gamma, h_res, mod3, cos, sin, pre_mix_g[l].reshape(1, d),
            ret_gn_g[l].reshape(1, RET_V), w_in_bf, late_w, seq)
        sb = _stick_breaking(q_s, k_s, v_s, bsz, seq)
        h_res = _post(h_res, mod3, ret, sb, a_r, a_s,
                      post_mix_g[l].reshape(1, d), pre_ffn_g[l].reshape(1, d),
                      post_ffn_g[l].reshape(1, d), *late_wb, seq)
    return h_res.reshape(bsz, seq, d).astype(x.dtype)
```

```python
import functools

import jax
import jax.numpy as jnp
from jax import lax
from jax.experimental import pallas as pl
from jax.experimental.pallas import tpu as pltpu

D_MODEL = 1024
CHUNK = 64
RET_HEADS = 8
RET_DQK = 64
RET_DV = 128
SB_HEADS = 8
SB_DH = 64
D_FF = 4 * D_MODEL
ROPE_BASE = 10000.0
EPS = 1e-6

RET_QK = RET_HEADS * RET_DQK
RET_V = RET_HEADS * RET_DV
SB_W = SB_HEADS * SB_DH
D_IN = 2 * RET_QK + 2 * RET_V + 3 * SB_W + 2 * D_MODEL

LANES = 128
BF16_ROWS = 16
N_LATE_W = 5
VMEM_LIMIT = 56 * 1024 * 1024

TM_IN = 512
TM_POST = 512
POST_SPLIT = 2
POST_FC = 1024
RET_L = 256
SB_T = 256
SB_SKIP_LOG = 105.0
SB_SKEW = 1
LOG2E = 1.4426950408889634

F32 = jnp.float32
BF16 = jnp.bfloat16


def _dot(a, b):
    return jnp.dot(a, b, preferred_element_type=F32)


def _dot_nt(a, b):
    return lax.dot_general(a, b, (((1,), (1,)), ((), ())), preferred_element_type=F32)


def _dot_tn(a, b):
    return lax.dot_general(a, b, (((0,), (0,)), ((), ())), preferred_element_type=F32)


def _rms(x, g):
    return x * lax.rsqrt(jnp.mean(x * x, axis=-1, keepdims=True) + EPS) * g


def _resident(shape):
    nd = len(shape)
    return pl.BlockSpec(shape, lambda *_: (0,) * nd, pipeline_mode=pl.Buffered(1))


def _ada_kernel(c_ref, w_ref, b_ref, o_ref):
    c = c_ref[...]
    s = (c * jax.nn.sigmoid(c)).astype(BF16)
    o_ref[...] = _dot(s, w_ref[...].astype(BF16)) + b_ref[...]


def _ada(c, w, b):
    bsz, d = c.shape
    n = w.shape[1]
    tn = 1024
    return pl.pallas_call(
        _ada_kernel,
        out_shape=jax.ShapeDtypeStruct((bsz, n), F32),
        grid=(n // tn,),
        in_specs=[pl.BlockSpec((bsz, d), lambda j: (0, 0)),
                  pl.BlockSpec((d, tn), lambda j: (0, j)),
                  pl.BlockSpec((1, tn), lambda j: (0, j))],
        out_specs=pl.BlockSpec((bsz, tn), lambda j: (0, j)),
        compiler_params=pltpu.CompilerParams(dimension_semantics=("arbitrary",)),
        name="ada",
    )(c, w, b.reshape(1, n))


def _rope_kernel(pos_ref, freq_ref, cos_ref, sin_ref):
    ang = freq_ref[...] * pos_ref[0].astype(F32)
    cos = jnp.cos(ang)
    sin = jnp.sin(ang)
    pairs = LANES // RET_DQK
    cos_ref[...] = jnp.concatenate([cos, cos] * pairs, axis=0).T
    sin_ref[...] = jnp.concatenate([-sin, sin] * pairs, axis=0).T


def _rope_tables(positions, inv_freq):
    bsz, seq = positions.shape
    nf = inv_freq.shape[0]
    return pl.pallas_call(
        _rope_kernel,
        out_shape=[jax.ShapeDtypeStruct((bsz * seq, LANES), F32)] * 2,
        grid=(bsz,),
        in_specs=[pl.BlockSpec((1, 1, seq), lambda b: (b, 0, 0)),
                  pl.BlockSpec((nf, 1), lambda b: (0, 0))],
        out_specs=[pl.BlockSpec((seq, LANES), lambda b: (b, 0))] * 2,
        compiler_params=pltpu.CompilerParams(dimension_semantics=("arbitrary",)),
        name="rope",
    )(positions.reshape(bsz, 1, seq), inv_freq.reshape(nf, 1))


def _inproj_kernel(lg_ref, x_ref, mod_ref, cos_ref, sin_ref, g_ref, gn_ref, w_ref, *refs,
                   tiles_per_seq):
    late_w = refs[:N_LATE_W]
    ret_ref, qs_ref, ks_ref, vs_ref, ar_ref, as_ref = refs[N_LATE_W:N_LATE_W + 6]
    late_wb = refs[N_LATE_W + 6:2 * N_LATE_W + 6]
    (qr_ref, kr_ref, vr_ref, gr_ref,
     state_ref, dtab_ref, qdec_ref, kdec_ref) = refs[2 * N_LATE_W + 6:]
    i = pl.program_id(0)
    blk = RET_L

    for src, dst in zip(late_w, late_wb):
        dst[...] = src[...].astype(BF16)

    @pl.when(i == 0)
    def _():
        ri = lax.broadcasted_iota(jnp.int32, (blk, blk), 0)
        ci = lax.broadcasted_iota(jnp.int32, (blk, blk), 1)
        dist = jnp.abs(ri - ci).astype(F32)
        keep = (ci // CHUNK) <= (ri // CHUNK)
        idx = lax.broadcasted_iota(jnp.int32, (blk, LANES), 0).astype(F32)
        for hd in range(RET_HEADS):
            dtab_ref[hd] = jnp.where(keep, jnp.exp(dist * lg_ref[hd]), 0.0)
            qdec_ref[hd] = jnp.exp((idx + 1.0) * lg_ref[hd])
            kdec_ref[hd] = jnp.exp((blk - 1.0 - idx) * lg_ref[hd])

    @pl.when(lax.rem(i, tiles_per_seq) == 0)
    def _():
        state_ref[...] = jnp.zeros_like(state_ref)

    h = _rms(x_ref[...], g_ref[...])
    h = h * (1.0 + mod_ref[0, 1:2, :]) + mod_ref[0, 0:1, :]
    hb = h.astype(BF16)

    cos = cos_ref[...]
    sin = sin_ref[...]
    lane = lax.broadcasted_iota(jnp.int32, (1, LANES), 1)
    first_half = (lane % RET_DQK) < (RET_DQK // 2)

    def rope(p):
        rot = jnp.where(first_half, pltpu.roll(p, LANES - RET_DQK // 2, 1),
                        pltpu.roll(p, RET_DQK // 2, 1))
        return p * cos + rot * sin

    def proj(c0, width):
        return _dot(hb, w_ref[:, c0:c0 + width].astype(BF16))

    c0 = 0
    acc = proj(c0, RET_QK)
    for j in range(RET_QK // LANES):
        qr_ref[:, j * LANES:(j + 1) * LANES] = rope(acc[:, j * LANES:(j + 1) * LANES]).astype(BF16)
    c0 += RET_QK
    acc = proj(c0, RET_QK)
    kscale = RET_DQK ** -0.5
    for j in range(RET_QK // LANES):
        kr_ref[:, j * LANES:(j + 1) * LANES] = (
            rope(acc[:, j * LANES:(j + 1) * LANES]) * kscale).astype(BF16)
    c0 += RET_QK
    for ref, width in ((vr_ref, RET_V), (gr_ref, RET_V)):
        ref[...] = proj(c0, width).astype(BF16)
        c0 += width

    def retention_block(rows):
        for hd in range(RET_HEADS):
            p = hd // 2
            in_head = (lane // RET_DQK) == (hd % 2)
            qp = qr_ref[rows, p * LANES:(p + 1) * LANES]
            kp = kr_ref[rows, p * LANES:(p + 1) * LANES]
            vh = vr_ref[rows, hd * RET_DV:(hd + 1) * RET_DV]
            qh = jnp.where(in_head, qp, jnp.zeros_like(qp))
            st = state_ref[hd]

            scores = _dot_nt(qh, kp) * dtab_ref[hd]
            intra = _dot(scores.astype(BF16), vh)
            cross = _dot(qh, st.astype(BF16)) * qdec_ref[hd]
            kd = (kp.astype(F32) * kdec_ref[hd]).astype(BF16)
            state_ref[hd] = st * jnp.exp(blk * lg_ref[hd]) + _dot_tn(kd, vh)

            o = intra + cross
            mu = jnp.mean(o, axis=-1, keepdims=True)
            dlt = o - mu
            var = jnp.mean(dlt * dlt, axis=-1, keepdims=True)
            y = dlt * lax.rsqrt(var + EPS) * gn_ref[:, hd * RET_DV:(hd + 1) * RET_DV]
            g = gr_ref[rows, hd * RET_DV:(hd + 1) * RET_DV].astype(F32)
            ret_ref[rows, hd * RET_DV:(hd + 1) * RET_DV] = (g * jax.nn.sigmoid(g) * y).astype(BF16)

    rest = ((qs_ref, SB_W), (ks_ref, SB_W), (vs_ref, SB_W), (ar_ref, D_MODEL), (as_ref, D_MODEL))
    nblk = x_ref.shape[0] // blk
    assert nblk <= len(rest)
    for n, (ref, width) in enumerate(rest):
        if n < nblk:
            retention_block(pl.ds(n * blk, blk))
        ref[...] = proj(c0, width).astype(BF16)
        c0 += width


def _inproj(log_gamma, x2, mod3, cos, sin, g, gn, w, late_w, seq):
    t, d = x2.shape
    tm = TM_IN
    steps = t // tm
    tiles_per_seq = seq // tm
    widths = (RET_V, SB_W, SB_W, SB_W, D_MODEL, D_MODEL)
    row = lambda i: (i, 0)
    assert len(late_w) == N_LATE_W
    slabs = []
    for lw in late_w:
        assert lw.shape[0] % (steps * BF16_ROWS) == 0
        slabs.append(pl.BlockSpec((lw.shape[0] // steps, lw.shape[1]), row))
    return pl.pallas_call(
        functools.partial(_inproj_kernel, tiles_per_seq=tiles_per_seq),
        out_shape=[jax.ShapeDtypeStruct((t, n), BF16) for n in widths]
        + [jax.ShapeDtypeStruct(lw.shape, BF16) for lw in late_w],
        grid=(steps,),
        in_specs=[pl.BlockSpec(memory_space=pltpu.SMEM),
                  pl.BlockSpec((tm, d), row),
                  pl.BlockSpec((1, 6, d), lambda i: (i // tiles_per_seq, 0, 0)),
                  pl.BlockSpec((tm, LANES), row),
                  pl.BlockSpec((tm, LANES), row),
                  _resident((1, d)),
                  _resident((1, RET_V)),
                  _resident((d, D_IN))] + slabs,
        out_specs=[pl.BlockSpec((tm, w), row) for w in widths] + slabs,
        scratch_shapes=[pltpu.VMEM((tm, RET_QK), BF16), pltpu.VMEM((tm, RET_QK), BF16),
                        pltpu.VMEM((tm, RET_V), BF16), pltpu.VMEM((tm, RET_V), BF16),
                        pltpu.VMEM((RET_HEADS, LANES, RET_DV), F32),
                        pltpu.VMEM((RET_HEADS, RET_L, RET_L), F32),
                        pltpu.VMEM((RET_HEADS, RET_L, LANES), F32),
                        pltpu.VMEM((RET_HEADS, RET_L, LANES), F32)],
        compiler_params=pltpu.CompilerParams(dimension_semantics=("arbitrary",),
                                             vmem_limit_bytes=VMEM_LIMIT),
        name="inproj",
    )(log_gamma, x2, mod3, cos, sin, g, gn, w, *late_w)


def _sb_kernel(q_ref, k_ref, v_ref, o_ref, qm_ref, tri_ref, acc_ref, r_ref):
    qi = pl.program_id(1)
    t = SB_T
    lane = lax.broadcasted_iota(jnp.int32, (1, LANES), 1)

    @pl.when(jnp.logical_and(pl.program_id(0) == 0, qi == 0))
    def _():
        ri = lax.broadcasted_iota(jnp.int32, (t, t), 0)
        ci = lax.broadcasted_iota(jnp.int32, (t, t), 1)
        tri_ref[...] = jnp.where(ri > ci, 1.0, 0.0).astype(BF16)

    for h in range(SB_HEADS):
        qp = q_ref[:, (h // 2) * LANES:(h // 2 + 1) * LANES]
        in_head = (lane // SB_DH) == (h % 2)
        qm_ref[h] = jnp.where(in_head, qp, jnp.zeros_like(qp)) * (SB_DH ** -0.5)

    ri = lax.broadcasted_iota(jnp.int32, (t, t), 0)
    ci = lax.broadcasted_iota(jnp.int32, (t, t), 1)
    valid = ci < ri

    def stage_scores(h, start, diag):
        cols = slice((h // 2) * LANES, (h // 2 + 1) * LANES)
        z = _dot_nt(qm_ref[h], k_ref[pl.ds(start, t), cols])
        sp = jnp.maximum(z, 0.0) + jnp.log(1.0 + jnp.exp2(jnp.abs(z) * (-LOG2E)))
        d = z - sp
        if diag:
            sp = jnp.where(valid, sp, 0.0)
        else:
            d = d - jnp.concatenate([r_ref[h]] * (t // LANES), axis=-1)
        rs = jnp.broadcast_to(jnp.sum(sp, axis=-1, keepdims=True), (t, LANES))
        if diag:
            r_ref[h] = rs
        else:
            r_ref[h] += rs
        return sp.astype(BF16), d

    def stage_weights(spb, d, diag):
        a = jnp.exp(d - _dot(spb, tri_ref[...]))
        if diag:
            a = jnp.where(valid, a, 0.0)
        return a.astype(BF16)

    def stage_values(h, ab, start, diag):
        cols = slice((h // 2) * LANES, (h // 2 + 1) * LANES)
        pv = _dot(ab, v_ref[pl.ds(start, t), cols])
        if diag:
            acc_ref[h] = pv
        else:
            acc_ref[h] += pv

    def sweep(blocks):
        items = [(h, start, diag) for start, diag in blocks for h in range(SB_HEADS)]
        s1 = {}
        s2 = {}
        for step in range(len(items) + 2 * SB_SKEW):
            if step < len(items):
                h, start, diag = items[step]
                s1[step] = stage_scores(h, start, diag)
            n = step - SB_SKEW
            if 0 <= n < len(items):
                s2[n] = stage_weights(*s1.pop(n), items[n][2])
            n = step - 2 * SB_SKEW
            if 0 <= n < len(items):
                h, start, diag = items[n]
                stage_values(h, s2.pop(n), start, diag)

    def key_block(it):
        return pl.multiple_of((qi - 1 - it) * t, t)

    diag_block = (pl.multiple_of(qi * t, t), True)

    @pl.when(qi == 0)
    def _():
        sweep([diag_block])

    @pl.when(qi > 0)
    def _():
        sweep([diag_block, (key_block(0), False)])

    def all_rows_done():
        r_min = r_ref[0]
        for h in range(1, SB_HEADS):
            r_min = jnp.minimum(r_min, r_ref[h])
        return jnp.min(r_min) >= SB_SKIP_LOG

    def cond(carry):
        it, done = carry
        return jnp.logical_and(it < qi, jnp.logical_not(done))

    def body(carry):
        it, _ = carry
        sweep([(key_block(it), False)])
        return it + 1, all_rows_done()

    lax.while_loop(cond, body, (jnp.int32(1), all_rows_done()))

    for p in range(SB_HEADS // 2):
        o_ref[:, p * LANES:(p + 1) * LANES] = jnp.where(
            lane < SB_DH, acc_ref[2 * p], acc_ref[2 * p + 1]).astype(BF16)


def _stick_breaking(q, k, v, bsz, seq):
    t = q.shape[0]
    nq = seq // SB_T
    return pl.pallas_call(
        _sb_kernel,
        out_shape=jax.ShapeDtypeStruct((t, SB_W), BF16),
        grid=(bsz, nq),
        in_specs=[pl.BlockSpec((SB_T, SB_W), lambda b, i: (b * nq + i, 0)),
                  pl.BlockSpec((seq, SB_W), lambda b, i: (b, 0)),
                  pl.BlockSpec((seq, SB_W), lambda b, i: (b, 0))],
        out_specs=pl.BlockSpec((SB_T, SB_W), lambda b, i: (b * nq + i, 0)),
        scratch_shapes=[pltpu.VMEM((SB_HEADS, SB_T, LANES), BF16),
                        pltpu.VMEM((SB_T, SB_T), BF16),
                        pltpu.VMEM((SB_HEADS, SB_T, LANES), F32),
                        pltpu.VMEM((SB_HEADS, SB_T, LANES), F32)],
        compiler_params=pltpu.CompilerParams(dimension_semantics=("arbitrary", "arbitrary"),
                                             vmem_limit_bytes=VMEM_LIMIT),
        name="stickbreak",
    )(q, k, v)


def _post_kernel(x_ref, mod_ref, ret_ref, sb_ref, ar_ref, as_ref,
                 g_mix_ref, g_pre_ref, g_post_ref,
                 w_ret_ref, w_sb_ref, w_out_ref, w_ff1_ref, w_ff2_ref, o_ref):
    gt1 = mod_ref[0, 2:3, :]
    sh2 = mod_ref[0, 3:4, :]
    sc2 = mod_ref[0, 4:5, :]
    gt2 = mod_ref[0, 5:6, :]

    def mix(rows):
        mixed = (jax.nn.sigmoid(ar_ref[rows, :].astype(F32)) * _dot(ret_ref[rows, :], w_ret_ref[...])
                 + jax.nn.sigmoid(as_ref[rows, :].astype(F32)) * _dot(sb_ref[rows, :], w_sb_ref[...]))
        y = _dot(mixed.astype(BF16), w_out_ref[...])
        h_res = x_ref[rows, :] + gt1 * _rms(y, g_mix_ref[...])
        h2 = (_rms(h_res, g_pre_ref[...]) * (1.0 + sc2) + sh2).astype(BF16)
        return h_res, h2

    def mlp(h2):
        f = None
        fc = POST_FC
        for c in range(D_FF // fc):
            u = jnp.maximum(_dot(h2, w_ff1_ref[:, c * fc:(c + 1) * fc]), 0.0)
            part = _dot((u * u).astype(BF16), w_ff2_ref[c * fc:(c + 1) * fc, :])
            f = part if f is None else f + part
        return f

    tm = x_ref.shape[0]
    groups = [pl.ds(i * (tm // POST_SPLIT), tm // POST_SPLIT) for i in range(POST_SPLIT)]
    mixed = [mix(rows) for rows in groups]
    for rows, (h_res, h2) in zip(groups, mixed):
        o_ref[rows, :] = h_res + gt2 * _rms(mlp(h2), g_post_ref[...])


def _post(x2, mod3, ret, sb, a_r, a_s, g_mix, g_pre, g_post, w_ret, w_sb, w_out, w_ff1, w_ff2, seq):
    t, d = x2.shape
    tm = TM_POST
    tiles_per_seq = seq // tm
    row = lambda i: (i, 0)
    return pl.pallas_call(
        _post_kernel,
        out_shape=jax.ShapeDtypeStruct((t, d), F32),
        grid=(t // tm,),
        in_specs=[pl.BlockSpec((tm, d), row),
                  pl.BlockSpec((1, 6, d), lambda i: (i // tiles_per_seq, 0, 0)),
                  pl.BlockSpec((tm, RET_V), row),
                  pl.BlockSpec((tm, SB_W), row),
                  pl.BlockSpec((tm, d), row),
                  pl.BlockSpec((tm, d), row),
                  _resident((1, d)), _resident((1, d)), _resident((1, d)),
                  _resident((RET_V, d)), _resident((SB_W, d)), _resident((d, d)),
                  _resident((d, D_FF)), _resident((D_FF, d))],
        out_specs=pl.BlockSpec((tm, d), row),
        compiler_params=pltpu.CompilerParams(dimension_semantics=("arbitrary",),
                                             vmem_limit_bytes=VMEM_LIMIT),
        name="post",
    )(x2, mod3, ret, sb, a_r, a_s, g_mix, g_pre, g_post, w_ret, w_sb, w_out, w_ff1, w_ff2)


def kernel(x, c, positions, ada_w, ada_b, pre_mix_g, post_mix_g, pre_ffn_g, post_ffn_g,
           w_in, ret_gn_g, w_ret_branch, w_sb_branch, w_out, w_ff1, w_ff2):
    bsz, seq, d = x.shape
    depth = ada_w.shape[0]
    assert d == D_MODEL and seq % TM_IN == 0 and seq % TM_POST == 0
    assert TM_IN % RET_L == 0 and seq % SB_T == 0 and RET_L % CHUNK == 0
    t = bsz * seq

    inv_freq = ROPE_BASE ** (-jnp.arange(0, RET_DQK, 2, dtype=F32) / RET_DQK)
    log_gamma = jnp.log1p(-(2.0 ** (-5.0 - jnp.arange(RET_HEADS, dtype=F32))))
    cos, sin = _rope_tables(positions, inv_freq)

    h_res = x.astype(F32).reshape(t, d)
    for l in range(depth):
        mod3 = _ada(c.astype(F32), ada_w[l], ada_b[l]).reshape(bsz, 6, d)
        late_w = (w_ret_branch[l], w_sb_branch[l], w_out[l], w_ff1[l], w_ff2[l])
        ret, q_s, k_s, v_s, a_r, a_s, *late_wb = _inproj(
            log_gamma, h_res, mod3, cos, sin, pre_mix_g[l].reshape(1, d),
            ret_gn_g[l].reshape(1, RET_V), w_in[l], late_w, seq)
        sb = _stick_breaking(q_s, k_s, v_s, bsz, seq)
        h_res = _post(h_res, mod3, ret, sb, a_r, a_s,
                      post_mix_g[l].reshape(1, d), pre_ffn_g[l].reshape(1, d),
                      post_ffn_g[l].reshape(1, d), *late_wb, seq)
    return h_res.reshape(bsz, seq, d).astype(x.dtype)
```
